```python
import math
import jax, jax.numpy as jnp
from jax import lax
import numpy as np

D_MODEL = 1024
BATCH = 1
SEQ = 16384
DEPTH = 1

N_MEM = 256
D_MIX = D_MODEL
D_A = D_MIX // 2
D_B = D_MIX - D_A
CONV_A_W = 3
CONV_B_W = 31
D_IN_ALL = 3 * D_A + 2 * D_B
XA_HEADS = 4
XA_HEAD_DIM = D_MODEL // XA_HEADS
D_FF = int(math.ceil((8 * D_MODEL / 3) / 256) * 256)
RMS_EPS = 1e-6
LN_EPS = 1e-5

kernel_name = "hybrid_parallel_conv_groups_xattn_swiglu"


def rmsnorm(x, g):
    xf = x.astype(jnp.float32)
    y = xf * lax.rsqrt(jnp.mean(xf * xf, axis=-1, keepdims=True) + RMS_EPS)
    return (y * g.astype(jnp.float32)).astype(x.dtype)


def layernorm(x, g, b):
    xf = x.astype(jnp.float32)
    mu = jnp.mean(xf, axis=-1, keepdims=True)
    var = jnp.mean(jnp.square(xf - mu), axis=-1, keepdims=True)
    y = (xf - mu) * lax.rsqrt(var + LN_EPS)
    return (y * g.astype(jnp.float32) + b.astype(jnp.float32)).astype(x.dtype)


def causal_dwconv(u, w):
    k = w.shape[0]
    return lax.conv_general_dilated(
        u, w[:, None, :].astype(u.dtype),
        window_strides=(1,), padding=((k - 1, 0),),
        dimension_numbers=("NWC", "WIO", "NWC"),
        feature_group_count=u.shape[-1])


def setup_inputs(seed: int = 0) -> dict:
    key = jax.random.key(seed)
    ks = jax.random.split(key, 24)
    f32 = jnp.float32

    def w(k, shape, fan_in):
        return jax.random.normal(k, shape, f32) * (fan_in ** -0.5)

    def gain(k, n):
        return jnp.ones((n,), f32) + 0.05 * jax.random.normal(k, (n,), f32)

    return {
        "x": jax.random.normal(ks[0], (BATCH, SEQ, D_MODEL), f32),
        "mem": jax.random.normal(ks[1], (BATCH, N_MEM, D_MODEL), f32),
        "mix_pre_g": gain(ks[2], D_MODEL),
        "w_mix_in": w(ks[3], (D_MODEL, D_IN_ALL), D_MODEL),
        "conv_a_w": w(ks[4], (CONV_A_W, D_A), CONV_A_W),
        "conv_b_w": w(ks[5], (CONV_B_W, D_B), CONV_B_W),
        "conv_b_b": 0.02 * jax.random.normal(ks[6], (D_B,), f32),
        "ln_b_g": gain(ks[7], D_B),
        "ln_b_b": 0.02 * jax.random.normal(ks[8], (D_B,), f32),
        "w_mix_out": w(ks[9], (D_MIX, D_MODEL), D_MIX),
        "mix_post_g": gain(ks[10], D_MODEL),
        "xa_pre_g": gain(ks[11], D_MODEL),
        "mem_norm_g": gain(ks[12], D_MODEL),
        "w_q": w(ks[13], (D_MODEL, XA_HEADS * XA_HEAD_DIM), D_MODEL),
        "w_k": w(ks[14], (D_MODEL, XA_HEADS * XA_HEAD_DIM), D_MODEL),
        "w_v": w(ks[15], (D_MODEL, XA_HEADS * XA_HEAD_DIM), D_MODEL),
        "w_o": w(ks[16], (XA_HEADS * XA_HEAD_DIM, D_MODEL), XA_HEADS * XA_HEAD_DIM),
        "xa_post_g": gain(ks[17], D_MODEL),
        "ffn_pre_g": gain(ks[18], D_MODEL),
        "w_gate": w(ks[19], (D_MODEL, D_FF), D_MODEL),
        "w_up": w(ks[20], (D_MODEL, D_FF), D_MODEL),
        "w_down": w(ks[21], (D_FF, D_MODEL), D_FF),
        "ffn_post_g": gain(ks[22], D_MODEL),
    }


def parallel_conv_mixer(h, w_mix_in, conv_a_w, conv_b_w, conv_b_b, ln_b_g, ln_b_b, w_mix_out):
    u = jnp.einsum("bsd,dc->bsc", h, w_mix_in.astype(h.dtype))
    b_a, c_a, v_a, glu_v, glu_g = jnp.split(
        u, [D_A, 2 * D_A, 3 * D_A, 3 * D_A + D_B], axis=-1)
    y_a = b_a * causal_dwconv(c_a * v_a, conv_a_w)
    z = glu_v * jax.nn.sigmoid(glu_g)
    z = causal_dwconv(z, conv_b_w) + conv_b_b.astype(z.dtype)
    y_b = jax.nn.silu(layernorm(z, ln_b_g, ln_b_b))
    y = jnp.concatenate([y_a, y_b], axis=-1)
    return jnp.einsum("bsc,cd->bsd", y, w_mix_out.astype(y.dtype))


def memory_cross_attention(h, mem_n, w_q, w_k, w_v, w_o):
    bsz, s, _ = h.shape
    m = mem_n.shape[1]
    q = jnp.einsum("bsd,de->bse", h, w_q.astype(h.dtype)).reshape(bsz, s, XA_HEADS, XA_HEAD_DIM)
    k = jnp.einsum("bmd,de->bme", mem_n, w_k.astype(h.dtype)).reshape(bsz, m, XA_HEADS, XA_HEAD_DIM)
    v = jnp.einsum("bmd,de->bme", mem_n, w_v.astype(h.dtype)).reshape(bsz, m, XA_HEADS, XA_HEAD_DIM)
    scores = jnp.einsum("bshd,bmhd->bhsm", q.astype(jnp.float32), k.astype(jnp.float32))
    p = jax.nn.softmax(scores * (XA_HEAD_DIM ** -0.5), axis=-1).astype(h.dtype)
    o = jnp.einsum("bhsm,bmhd->bshd", p, v).reshape(bsz, s, XA_HEADS * XA_HEAD_DIM)
    return jnp.einsum("bse,ed->bsd", o, w_o.astype(h.dtype))


def swiglu_ffn(h, w_gate, w_up, w_down):
    g = jnp.einsum("bsd,df->bsf", h, w_gate.astype(h.dtype))
    u = jnp.einsum("bsd,df->bsf", h, w_up.astype(h.dtype))
    return jnp.einsum("bsf,fd->bsd", jax.nn.silu(g) * u, w_down.astype(h.dtype))


def reference(x, mem, mix_pre_g, w_mix_in, conv_a_w, conv_b_w, conv_b_b, ln_b_g, ln_b_b,
              w_mix_out, mix_post_g, xa_pre_g, mem_norm_g, w_q, w_k, w_v, w_o, xa_post_g,
              ffn_pre_g, w_gate, w_up, w_down, ffn_post_g):
    mem_n = rmsnorm(mem, mem_norm_g)
    for _ in range(DEPTH):
        x = x + rmsnorm(parallel_conv_mixer(rmsnorm(x, mix_pre_g), w_mix_in, conv_a_w,
                                            conv_b_w, conv_b_b, ln_b_g, ln_b_b, w_mix_out),
                        mix_post_g)
        x = x + rmsnorm(memory_cross_attention(rmsnorm(x, xa_pre_g), mem_n, w_q, w_k, w_v, w_o),
                        xa_post_g)
        x = x + rmsnorm(swiglu_ffn(rmsnorm(x, ffn_pre_g), w_gate, w_up, w_down), ffn_post_g)
    return x
```

```python
import functools

import jax
import jax.numpy as jnp
from jax import lax
from jax.experimental import pallas as pl
from jax.experimental.pallas import tpu as pltpu

RMS_EPS = 1e-6
LN_EPS = 1e-5
CONV_A_W = 3
CONV_B_W = 31
XA_HEADS = 4

ROW_TILE = 512
HALO_A = 8
HALO_B = 32
VMEM_LIMIT_BYTES = 56 * 1024 * 1024

F32 = jnp.float32
BF16 = jnp.bfloat16


def _rmsnorm(x, g):
    ms = jnp.mean(x * x, axis=-1, keepdims=True)
    return x * lax.rsqrt(ms + RMS_EPS) * g


def _mixer_kernel(x_ref, pre_g_ref, w_in_ref, ca_w_ref, cb_w_ref, cb_b_ref, ln_g_ref, ln_b_ref,
                  w_out_ref, post_g_ref, o_ref, cv_buf, z_buf):
    tm = x_ref.shape[0]
    d_a = cv_buf.shape[1]

    @pl.when(pl.program_id(0) == 0)
    def _():
        cv_buf[0:HALO_A, :] = jnp.zeros((HALO_A, d_a), F32)
        z_buf[0:HALO_B, :] = jnp.zeros((HALO_B, z_buf.shape[1]), F32)

    h = _rmsnorm(x_ref[...], pre_g_ref[...]).astype(BF16)
    u = jnp.dot(h, w_in_ref[...], preferred_element_type=F32)
    b_a = u[:, 0:d_a]
    c_a = u[:, d_a:2 * d_a]
    v_a = u[:, 2 * d_a:3 * d_a]
    glu_v = u[:, 3 * d_a:4 * d_a]
    glu_g = u[:, 4 * d_a:5 * d_a]

    cv_buf[HALO_A:HALO_A + tm, :] = c_a * v_a
    conv_a = None
    for k in range(CONV_A_W):
        off = HALO_A - (CONV_A_W - 1) + k
        term = ca_w_ref[k:k + 1, :] * cv_buf[off:off + tm, :]
        conv_a = term if conv_a is None else conv_a + term
    y_a = b_a * conv_a

    z_buf[HALO_B:HALO_B + tm, :] = glu_v * jax.nn.sigmoid(glu_g)
    conv_b = None
    for k in range(CONV_B_W):
        off = HALO_B - (CONV_B_W - 1) + k
        term = cb_w_ref[k:k + 1, :] * z_buf[off:off + tm, :]
        conv_b = term if conv_b is None else conv_b + term
    conv_b = conv_b + cb_b_ref[...]
    mu = jnp.mean(conv_b, axis=-1, keepdims=True)
    cen = conv_b - mu
    var = jnp.mean(cen * cen, axis=-1, keepdims=True)
    ln = cen * lax.rsqrt(var + LN_EPS) * ln_g_ref[...] + ln_b_ref[...]
    y_b = ln * jax.nn.sigmoid(ln)

    y = jnp.concatenate([y_a.astype(BF16), y_b.astype(BF16)], axis=-1)
    proj = jnp.dot(y, w_out_ref[...], preferred_element_type=F32)
    o_ref[...] = x_ref[...] + _rmsnorm(proj, post_g_ref[...])

    cv_buf[0:HALO_A, :] = cv_buf[tm:tm + HALO_A, :]
    z_buf[0:HALO_B, :] = z_buf[tm:tm + HALO_B, :]


def _kv_kernel(mem_ref, g_ref, w_k_ref, w_v_ref, kt_ref, v_ref):
    mem_n = _rmsnorm(mem_ref[...], g_ref[...]).astype(BF16)
    k = jnp.dot(mem_n, w_k_ref[...].astype(BF16), preferred_element_type=F32)
    v = jnp.dot(mem_n, w_v_ref[...].astype(BF16), preferred_element_type=F32)
    kt_ref[...] = k.T.astype(BF16)
    v_ref[...] = v.astype(BF16)


def _xattn_kernel(x_ref, pre_g_ref, w_q_ref, kt_ref, v_ref, w_o_ref, post_g_ref, o_ref):
    hd = kt_ref.shape[0] // XA_HEADS
    h = _rmsnorm(x_ref[...], pre_g_ref[...]).astype(BF16)
    q = jnp.dot(h, w_q_ref[...], preferred_element_type=F32)
    q = (q * (hd ** -0.5)).astype(BF16)
    outs = []
    for i in range(XA_HEADS):
        sl = slice(i * hd, (i + 1) * hd)
        s = jnp.dot(q[:, sl], kt_ref[sl, :], preferred_element_type=F32)
        e = jnp.exp(s - jnp.max(s, axis=-1, keepdims=True))
        p = e * (1.0 / jnp.sum(e, axis=-1, keepdims=True))
        outs.append(jnp.dot(p.astype(BF16), v_ref[:, sl], preferred_element_type=F32).astype(BF16))
    o = jnp.concatenate(outs, axis=-1)
    proj = jnp.dot(o, w_o_ref[...], preferred_element_type=F32)
    o_ref[...] = x_ref[...] + _rmsnorm(proj, post_g_ref[...])


def _ffn_kernel(x_ref, pre_g_ref, w_gate_ref, w_up_ref, w_down_ref, post_g_ref, o_ref):
    h = _rmsnorm(x_ref[...], pre_g_ref[...]).astype(BF16)
    g = jnp.dot(h, w_gate_ref[...], preferred_element_type=F32)
    u = jnp.dot(h, w_up_ref[...], preferred_element_type=F32)
    a = (g * jax.nn.sigmoid(g) * u).astype(BF16)
    proj = jnp.dot(a, w_down_ref[...], preferred_element_type=F32)
    o_ref[...] = x_ref[...] + _rmsnorm(proj, post_g_ref[...])


def _resident(shape):
    return pl.BlockSpec(shape, lambda i: (0,) * len(shape), pipeline_mode=pl.Buffered(1))


def _row_tiled_call(body, x, consts, scratch_shapes, name):
    s, d = x.shape
    assert s % ROW_TILE == 0
    row_spec = pl.BlockSpec((ROW_TILE, d), lambda i: (i, 0))
    return pl.pallas_call(
        body,
        grid=(s // ROW_TILE,),
        in_specs=[row_spec] + [_resident(c.shape) for c in consts],
        out_specs=row_spec,
        out_shape=jax.ShapeDtypeStruct((s, d), x.dtype),
        scratch_shapes=scratch_shapes,
        compiler_params=pltpu.CompilerParams(
            dimension_semantics=("arbitrary",), vmem_limit_bytes=VMEM_LIMIT_BYTES),
        name=name,
    )(x, *consts)


def _row(v):
    return v.reshape(1, -1).astype(F32)


def kernel(x, mem, mix_pre_g, w_mix_in, conv_a_w, conv_b_w, conv_b_b, ln_b_g, ln_b_b, w_mix_out,
           mix_post_g, xa_pre_g, mem_norm_g, w_q, w_k, w_v, w_o, xa_post_g, ffn_pre_g, w_gate, w_up,
           w_down, ffn_post_g):
    bsz, seq, d = x.shape
    assert bsz == 1 and mem.shape[0] == 1
    d_a = conv_a_w.shape[1]
    d_b = conv_b_w.shape[1]
    assert d_a == d_b and w_mix_in.shape[1] == 3 * d_a + 2 * d_b
    assert conv_a_w.shape[0] == CONV_A_W and conv_b_w.shape[0] == CONV_B_W
    x2 = x.reshape(seq, d)

    x2 = _row_tiled_call(
        _mixer_kernel, x2,
        [_row(mix_pre_g), w_mix_in.astype(BF16), conv_a_w, conv_b_w, _row(conv_b_b), _row(ln_b_g),
         _row(ln_b_b), w_mix_out.astype(BF16), _row(mix_post_g)],
        [pltpu.VMEM((HALO_A + ROW_TILE, d_a), F32), pltpu.VMEM((HALO_B + ROW_TILE, d_b), F32)],
        "mixer")

    n_mem = mem.shape[1]
    e = w_k.shape[1]
    kt, v = pl.pallas_call(
        _kv_kernel,
        out_shape=(jax.ShapeDtypeStruct((e, n_mem), BF16), jax.ShapeDtypeStruct((n_mem, e), BF16)),
        compiler_params=pltpu.CompilerParams(vmem_limit_bytes=VMEM_LIMIT_BYTES),
        name="kv_proj",
    )(mem.reshape(n_mem, d), _row(mem_norm_g), w_k, w_v)

    x2 = _row_tiled_call(
        _xattn_kernel, x2,
        [_row(xa_pre_g), w_q.astype(BF16), kt, v, w_o.astype(BF16), _row(xa_post_g)],
        [], "xattn")

    x2 = _row_tiled_call(
        _ffn_kernel, x2,
        [_row(ffn_pre_g), w_gate.astype(BF16), w_up.astype(BF16), w_down.astype(BF16),
         _row(ffn_post_g)],
        [], "ffn")
    return x2.reshape(bsz, seq, d)
```

```python
import functools

import jax
import jax.numpy as jnp
from jax import lax
from jax.experimental import pallas as pl
from jax.experimental.pallas import tpu as pltpu

RMS_EPS = 1e-6
LN_EPS = 1e-5
CONV_A_W = 3
CONV_B_W = 31
XA_HEADS = 4

ROW_TILE = 512
HALO_A = 8
HALO_B = 32
LANES = 128
ROW_PITCH = 2
CONV_ROWS = 128
VMEM_LIMIT_BYTES = 56 * 1024 * 1024

F32 = jnp.float32
BF16 = jnp.bfloat16


def _rmsnorm(x, g):
    ms = jnp.mean(x * x, axis=-1, keepdims=True)
    return x * lax.rsqrt(ms + RMS_EPS) * g


def _store_rows(buf, row0, val):
    rows = val.shape[0]
    for g in range(buf.shape[0]):
        buf[g, pl.ds(ROW_PITCH * row0, rows, stride=ROW_PITCH), :] = val[:, g * LANES:(g + 1) * LANES]


def _load_rows(buf, g, row0, rows):
    return buf[g, pl.ds(ROW_PITCH * row0, rows, stride=ROW_PITCH), :]


def _causal_dwconv(buf, w_ref, halo, tm):
    n_taps = w_ref.shape[0]
    cols = []
    for g in range(buf.shape[0]):
        lanes = slice(g * LANES, (g + 1) * LANES)
        chunks = []
        for r0 in range(0, tm, CONV_ROWS):
            acc = None
            for k in range(n_taps):
                win = _load_rows(buf, g, halo - (n_taps - 1) + k + r0, CONV_ROWS)
                term = w_ref[k:k + 1, lanes] * win
                acc = term if acc is None else acc + term
            chunks.append(acc)
        cols.append(jnp.concatenate(chunks, axis=0))
    return jnp.concatenate(cols, axis=-1)


def _mixer_kernel(x_ref, pre_g_ref, w_in_ref, ca_w_ref, cb_w_ref, cb_b_ref, ln_g_ref, ln_b_ref,
                  w_out_ref, post_g_ref, o_ref, cv_buf, z_buf):
    tm = x_ref.shape[0]
    d_a = ca_w_ref.shape[1]

    @pl.when(pl.program_id(0) == 0)
    def _():
        _store_rows(cv_buf, 0, jnp.zeros((HALO_A, d_a), F32))
        _store_rows(z_buf, 0, jnp.zeros((HALO_B, cb_w_ref.shape[1]), F32))

    h = _rmsnorm(x_ref[...], pre_g_ref[...]).astype(BF16)
    u = jnp.dot(h, w_in_ref[...], preferred_element_type=F32)
    b_a = u[:, 0:d_a]
    c_a = u[:, d_a:2 * d_a]
    v_a = u[:, 2 * d_a:3 * d_a]
    glu_v = u[:, 3 * d_a:4 * d_a]
    glu_g = u[:, 4 * d_a:5 * d_a]

    _store_rows(cv_buf, HALO_A, c_a * v_a)
    y_a = b_a * _causal_dwconv(cv_buf, ca_w_ref, HALO_A, tm)

    _store_rows(z_buf, HALO_B, glu_v * jax.nn.sigmoid(glu_g))
    conv_b = _causal_dwconv(z_buf, cb_w_ref, HALO_B, tm) + cb_b_ref[...]
    mu = jnp.mean(conv_b, axis=-1, keepdims=True)
    cen = conv_b - mu
    var = jnp.mean(cen * cen, axis=-1, keepdims=True)
    ln = cen * lax.rsqrt(var + LN_EPS) * ln_g_ref[...] + ln_b_ref[...]
    y_b = ln * jax.nn.sigmoid(ln)

    y = jnp.concatenate([y_a.astype(BF16), y_b.astype(BF16)], axis=-1)
    proj = jnp.dot(y, w_out_ref[...], preferred_element_type=F32)
    o_ref[...] = x_ref[...] + _rmsnorm(proj, post_g_ref[...])

    for buf, halo in ((cv_buf, HALO_A), (z_buf, HALO_B)):
        _store_rows(buf, 0, jnp.concatenate(
            [_load_rows(buf, g, tm, halo) for g in range(buf.shape[0])], axis=-1))


def _kv_kernel(mem_ref, g_ref, w_k_ref, w_v_ref, kt_ref, v_ref):
    mem_n = _rmsnorm(mem_ref[...], g_ref[...]).astype(BF16)
    k = jnp.dot(mem_n, w_k_ref[...].astype(BF16), preferred_element_type=F32)
    v = jnp.dot(mem_n, w_v_ref[...].astype(BF16), preferred_element_type=F32)
    kt_ref[...] = k.T.astype(BF16)
    v_ref[...] = v.astype(BF16)


def _xattn_kernel(x_ref, pre_g_ref, w_q_ref, kt_ref, v_ref, w_o_ref, post_g_ref, o_ref):
    hd = kt_ref.shape[0] // XA_HEADS
    h = _rmsnorm(x_ref[...], pre_g_ref[...]).astype(BF16)
    q = jnp.dot(h, w_q_ref[...], preferred_element_type=F32)
    q = (q * (hd ** -0.5)).astype(BF16)
    outs = []
    for i in range(XA_HEADS):
        sl = slice(i * hd, (i + 1) * hd)
        s = jnp.dot(q[:, sl], kt_ref[sl, :], preferred_element_type=F32)
        e = jnp.exp(s - jnp.max(s, axis=-1, keepdims=True))
        p = e * (1.0 / jnp.sum(e, axis=-1, keepdims=True))
        outs.append(jnp.dot(p.astype(BF16), v_ref[:, sl], preferred_element_type=F32).astype(BF16))
    o = jnp.concatenate(outs, axis=-1)
    proj = jnp.dot(o, w_o_ref[...], preferred_element_type=F32)
    o_ref[...] = x_ref[...] + _rmsnorm(proj, post_g_ref[...])


def _ffn_kernel(x_ref, pre_g_ref, w_gate_ref, w_up_ref, w_down_ref, post_g_ref, o_ref):
    h = _rmsnorm(x_ref[...], pre_g_ref[...]).astype(BF16)
    g = jnp.dot(h, w_gate_ref[...], preferred_element_type=F32)
    u = jnp.dot(h, w_up_ref[...], preferred_element_type=F32)
    a = (g * jax.nn.sigmoid(g) * u).astype(BF16)
    proj = jnp.dot(a, w_down_ref[...], preferred_element_type=F32)
    o_ref[...] = x_ref[...] + _rmsnorm(proj, post_g_ref[...])


def _resident(shape):
    return pl.BlockSpec(shape, lambda i: (0,) * len(shape), pipeline_mode=pl.Buffered(1))


def _row_tiled_call(body, x, consts, scratch_shapes, name):
    s, d = x.shape
    assert s % ROW_TILE == 0
    row_spec = pl.BlockSpec((ROW_TILE, d), lambda i: (i, 0))
    return pl.pallas_call(
        body,
        grid=(s // ROW_TILE,),
        in_specs=[row_spec] + [_resident(c.shape) for c in consts],
        out_specs=row_spec,
        out_shape=jax.ShapeDtypeStruct((s, d), x.dtype),
        scratch_shapes=scratch_shapes,
        compiler_params=pltpu.CompilerParams(
            dimension_semantics=("arbitrary",), vmem_limit_bytes=VMEM_LIMIT_BYTES),
        name=name,
    )(x, *consts)


def _row(v):
    return v.reshape(1, -1).astype(F32)


def kernel(x, mem, mix_pre_g, w_mix_in, conv_a_w, conv_b_w, conv_b_b, ln_b_g, ln_b_b, w_mix_out,
           mix_post_g, xa_pre_g, mem_norm_g, w_q, w_k, w_v, w_o, xa_post_g, ffn_pre_g, w_gate, w_up,
           w_down, ffn_post_g):
    bsz, seq, d = x.shape
    assert bsz == 1 and mem.shape[0] == 1
    d_a = conv_a_w.shape[1]
    d_b = conv_b_w.shape[1]
    assert d_a == d_b and w_mix_in.shape[1] == 3 * d_a + 2 * d_b
    assert conv_a_w.shape[0] == CONV_A_W and conv_b_w.shape[0] == CONV_B_W
    x2 = x.reshape(seq, d)

    x2 = _row_tiled_call(
        _mixer_kernel, x2,
        [_row(mix_pre_g), w_mix_in.astype(BF16), conv_a_w, conv_b_w, _row(conv_b_b), _row(ln_b_g),
         _row(ln_b_b), w_mix_out.astype(BF16), _row(mix_post_g)],
        [pltpu.VMEM((d_a // LANES, ROW_PITCH * (HALO_A + ROW_TILE), LANES), F32),
         pltpu.VMEM((d_b // LANES, ROW_PITCH * (HALO_B + ROW_TILE), LANES), F32)],
        "mixer")

    n_mem = mem.shape[1]
    e = w_k.shape[1]
    kt, v = pl.pallas_call(
        _kv_kernel,
        out_shape=(jax.ShapeDtypeStruct((e, n_mem), BF16), jax.ShapeDtypeStruct((n_mem, e), BF16)),
        compiler_params=pltpu.CompilerParams(vmem_limit_bytes=VMEM_LIMIT_BYTES),
        name="kv_proj",
    )(mem.reshape(n_mem, d), _row(mem_norm_g), w_k, w_v)

    x2 = _row_tiled_call(
        _xattn_kernel, x2,
        [_row(xa_pre_g), w_q.astype(BF16), kt, v, w_o.astype(BF16), _row(xa_post_g)],
        [], "xattn")

    x2 = _row_tiled_call(
        _ffn_kernel, x2,
        [_row(ffn_pre_g), w_gate.astype(BF16), w_up.astype(BF16), w_down.astype(BF16),
         _row(ffn_post_g)],
        [], "ffn")
    return x2.reshape(bsz, seq, d)
```

```python
import jax
import jax.numpy as jnp
from jax import lax
from jax.experimental import pallas as pl
from jax.experimental.pallas import tpu as pltpu

RMS_EPS = 1e-6
LN_EPS = 1e-5
CONV_A_W = 3
CONV_B_W = 31
XA_HEADS = 4

ROW_TILE = 512
HALO_A = 8
HALO_B = 32
LANES = 128
ROW_PITCH = 2
CONV_ROWS = 128
VMEM_LIMIT_BYTES = 56 * 1024 * 1024

F32 = jnp.float32
BF16 = jnp.bfloat16


def _rmsnorm(x, g):
    ms = jnp.mean(x * x, axis=-1, keepdims=True)
    return x * lax.rsqrt(ms + RMS_EPS) * g


def _store_rows(buf, g, row0, val):
    buf[g, pl.ds(ROW_PITCH * row0, val.shape[0], stride=ROW_PITCH), :] = val


def _load_rows(buf, g, row0, rows):
    return buf[g, pl.ds(ROW_PITCH * row0, rows, stride=ROW_PITCH), :]


def _causal_dwconv(buf, w_ref, halo, g, r0):
    n_taps = w_ref.shape[0]
    acc = None
    for k in range(n_taps):
        win = _load_rows(buf, g, halo - (n_taps - 1) + k + r0, CONV_ROWS)
        term = w_ref[k:k + 1, g * LANES:(g + 1) * LANES] * win
        acc = term if acc is None else acc + term
    return acc


def _mixer_kernel(x_ref, pre_g_ref, w_in_ref, ca_w_ref, cb_w_ref, cb_b_ref, ln_g_ref, ln_b_ref,
                  w_out_ref, post_g_ref, o_ref, h_scr, b_scr, cb_scr, y_scr, cv_buf, z_buf):
    tm = x_ref.shape[0]
    d_a = ca_w_ref.shape[1]
    n_groups = d_a // LANES

    @pl.when(pl.program_id(0) == 0)
    def _():
        for g in range(n_groups):
            _store_rows(cv_buf, g, 0, jnp.zeros((HALO_A, LANES), F32))
            _store_rows(z_buf, g, 0, jnp.zeros((HALO_B, LANES), F32))

    h_scr[...] = _rmsnorm(x_ref[...], pre_g_ref[...]).astype(BF16)

    def in_proj(*col_starts, width=LANES):
        w = jnp.concatenate([w_in_ref[:, c:c + width] for c in col_starts], axis=-1)
        return jnp.dot(h_scr[...], w, preferred_element_type=F32)

    def stage_glu(g):
        glu = in_proj(3 * d_a + g * LANES, 4 * d_a + g * LANES)
        _store_rows(z_buf, g, HALO_B, glu[:, :LANES] * jax.nn.sigmoid(glu[:, LANES:]))

    def stage_cv(g0):
        cv = in_proj(d_a + g0 * LANES, 2 * d_a + g0 * LANES, width=2 * LANES)
        for g in (g0, g0 + 1):
            lo = (g - g0) * LANES
            _store_rows(cv_buf, g, HALO_A, cv[:, lo:lo + LANES] * cv[:, 2 * LANES + lo:3 * LANES + lo])

    def stage_b():
        b_scr[...] = in_proj(0, width=d_a)

    after_group = {0: [lambda: stage_glu(1), lambda: stage_cv(0)],
                   1: [lambda: stage_glu(2), lambda: stage_cv(2)],
                   2: [lambda: stage_glu(3), stage_b],
                   3: []}

    stage_glu(0)
    for g in range(n_groups):
        for r0 in range(0, tm, CONV_ROWS):
            cb_scr[r0:r0 + CONV_ROWS, g * LANES:(g + 1) * LANES] = _causal_dwconv(z_buf, cb_w_ref, HALO_B, g, r0)
        for stage in after_group[g]:
            stage()

    for g in range(n_groups):
        for r0 in range(0, tm, CONV_ROWS):
            rows, lanes = slice(r0, r0 + CONV_ROWS), slice(g * LANES, (g + 1) * LANES)
            y_scr[rows, lanes] = (b_scr[rows, lanes] * _causal_dwconv(cv_buf, ca_w_ref, HALO_A, g, r0)).astype(BF16)

    cb = cb_scr[...] + cb_b_ref[...]
    mu = jnp.mean(cb, axis=-1, keepdims=True)
    cen = cb - mu
    var = jnp.mean(cen * cen, axis=-1, keepdims=True)
    ln = cen * lax.rsqrt(var + LN_EPS) * ln_g_ref[...] + ln_b_ref[...]
    y_scr[:, d_a:] = (ln * jax.nn.sigmoid(ln)).astype(BF16)

    proj = jnp.dot(y_scr[...], w_out_ref[...], preferred_element_type=F32)
    o_ref[...] = x_ref[...] + _rmsnorm(proj, post_g_ref[...])

    for buf, halo in ((cv_buf, HALO_A), (z_buf, HALO_B)):
        for g in range(n_groups):
            _store_rows(buf, g, 0, _load_rows(buf, g, tm, halo))


def _kv_kernel(mem_ref, g_ref, w_k_ref, w_v_ref, kt_ref, v_ref):
    mem_n = _rmsnorm(mem_ref[...], g_ref[...]).astype(BF16)
    k = jnp.dot(mem_n, w_k_ref[...].astype(BF16), preferred_element_type=F32)
    v = jnp.dot(mem_n, w_v_ref[...].astype(BF16), preferred_element_type=F32)
    kt_ref[...] = k.T.astype(BF16)
    v_ref[...] = v.astype(BF16)


def _xattn_kernel(x_ref, pre_g_ref, w_q_ref, kt_ref, v_ref, w_o_ref, post_g_ref, o_ref):
    hd = kt_ref.shape[0] // XA_HEADS
    h = _rmsnorm(x_ref[...], pre_g_ref[...]).astype(BF16)
    q = jnp.dot(h, w_q_ref[...], preferred_element_type=F32)
    q = (q * (hd ** -0.5)).astype(BF16)
    outs = []
    for i in range(XA_HEADS):
        sl = slice(i * hd, (i + 1) * hd)
        s = jnp.dot(q[:, sl], kt_ref[sl, :], preferred_element_type=F32)
        e = jnp.exp(s - jnp.max(s, axis=-1, keepdims=True))
        p = e * (1.0 / jnp.sum(e, axis=-1, keepdims=True))
        outs.append(jnp.dot(p.astype(BF16), v_ref[:, sl], preferred_element_type=F32).astype(BF16))
    o = jnp.concatenate(outs, axis=-1)
    proj = jnp.dot(o, w_o_ref[...], preferred_element_type=F32)
    o_ref[...] = x_ref[...] + _rmsnorm(proj, post_g_ref[...])


def _ffn_kernel(x_ref, pre_g_ref, w_gate_ref, w_up_ref, w_down_ref, post_g_ref, o_ref):
    h = _rmsnorm(x_ref[...], pre_g_ref[...]).astype(BF16)
    g = jnp.dot(h, w_gate_ref[...], preferred_element_type=F32)
    u = jnp.dot(h, w_up_ref[...], preferred_element_type=F32)
    a = (g * jax.nn.sigmoid(g) * u).astype(BF16)
    proj = jnp.dot(a, w_down_ref[...], preferred_element_type=F32)
    o_ref[...] = x_ref[...] + _rmsnorm(proj, post_g_ref[...])


def _resident(shape):
    return pl.BlockSpec(shape, lambda i: (0,) * len(shape), pipeline_mode=pl.Buffered(1))


def _row_tiled_call(body, x, consts, scratch_shapes, name):
    s, d = x.shape
    assert s % ROW_TILE == 0
    row_spec = pl.BlockSpec((ROW_TILE, d), lambda i: (i, 0))
    return pl.pallas_call(
        body,
        grid=(s // ROW_TILE,),
        in_specs=[row_spec] + [_resident(c.shape) for c in consts],
        out_specs=row_spec,
        out_shape=jax.ShapeDtypeStruct((s, d), x.dtype),
        scratch_shapes=scratch_shapes,
        compiler_params=pltpu.CompilerParams(
            dimension_semantics=("arbitrary",), vmem_limit_bytes=VMEM_LIMIT_BYTES),
        name=name,
    )(x, *consts)


def _row(v):
    return v.reshape(1, -1).astype(F32)


def kernel(x, mem, mix_pre_g, w_mix_in, conv_a_w, conv_b_w, conv_b_b, ln_b_g, ln_b_b, w_mix_out,
           mix_post_g, xa_pre_g, mem_norm_g, w_q, w_k, w_v, w_o, xa_post_g, ffn_pre_g, w_gate, w_up,
           w_down, ffn_post_g):
    bsz, seq, d = x.shape
    assert bsz == 1 and mem.shape[0] == 1
    d_a = conv_a_w.shape[1]
    d_b = conv_b_w.shape[1]
    assert d_a == d_b and w_mix_in.shape[1] == 3 * d_a + 2 * d_b and d_a % (2 * LANES) == 0
    assert conv_a_w.shape[0] == CONV_A_W and conv_b_w.shape[0] == CONV_B_W
    x2 = x.reshape(seq, d)

    x2 = _row_tiled_call(
        _mixer_kernel, x2,
        [_row(mix_pre_g), w_mix_in.astype(BF16), conv_a_w, conv_b_w, _row(conv_b_b), _row(ln_b_g),
         _row(ln_b_b), w_mix_out.astype(BF16), _row(mix_post_g)],
        [pltpu.VMEM((ROW_TILE, d), BF16),
         pltpu.VMEM((ROW_TILE, d_a), F32),
         pltpu.VMEM((ROW_TILE, d_b), F32),
         pltpu.VMEM((ROW_TILE, d_a + d_b), BF16),
         pltpu.VMEM((d_a // LANES, ROW_PITCH * (HALO_A + ROW_TILE), LANES), F32),
         pltpu.VMEM((d_b // LANES, ROW_PITCH * (HALO_B + ROW_TILE), LANES), F32)],
        "mixer")

    n_mem = mem.shape[1]
    e = w_k.shape[1]
    kt, v = pl.pallas_call(
        _kv_kernel,
        out_shape=(jax.ShapeDtypeStruct((e, n_mem), BF16), jax.ShapeDtypeStruct((n_mem, e), BF16)),
        compiler_params=pltpu.CompilerParams(vmem_limit_bytes=VMEM_LIMIT_BYTES),
        name="kv_proj",
    )(mem.reshape(n_mem, d), _row(mem_norm_g), w_k, w_v)

    x2 = _row_tiled_call(
        _xattn_kernel, x2,
        [_row(xa_pre_g), w_q.astype(BF16), kt, v, w_o.astype(BF16), _row(xa_post_g)],
        [], "xattn")

    x2 = _row_tiled_call(
        _ffn_kernel, x2,
        [_row(ffn_pre_g), w_gate.astype(BF16), w_up.astype(BF16), w_down.astype(BF16),
         _row(ffn_post_g)],
        [], "ffn")
    return x2.reshape(bsz, seq, d)
```

```python
import jax
import jax.numpy as jnp
from jax import lax
from jax.experimental import pallas as pl
from jax.experimental.pallas import tpu as pltpu

RMS_EPS = 1e-6
LN_EPS = 1e-5
CONV_A_W = 3
CONV_B_W = 31
XA_HEADS = 4

ROW_TILE = 1024
SUB_ROWS = 512
MIXER_ROW_TILE = SUB_ROWS
HALO_A = 8
HALO_B = 32
LANES = 128
ROW_PITCH = 2
CONV_ROWS = 128
FFN_COLS = 1024
VMEM_LIMIT_BYTES = 58 * 1024 * 1024

F32 = jnp.float32
BF16 = jnp.bfloat16


def _rmsnorm(x, g):
    ms = jnp.mean(x * x, axis=-1, keepdims=True)
    return x * lax.rsqrt(ms + RMS_EPS) * g


def _halves(tm):
    return [slice(r, r + SUB_ROWS) for r in range(0, tm, SUB_ROWS)]


def _store_rows(buf, g, row0, val):
    buf[g, pl.ds(ROW_PITCH * row0, val.shape[0], stride=ROW_PITCH), :] = val


def _load_rows(buf, g, row0, rows):
    return buf[g, pl.ds(ROW_PITCH * row0, rows, stride=ROW_PITCH), :]


def _causal_dwconv(buf, w_ref, halo, g, r0):
    n_taps = w_ref.shape[0]
    acc = None
    for k in range(n_taps):
        win = _load_rows(buf, g, halo - (n_taps - 1) + k + r0, CONV_ROWS)
        term = w_ref[k:k + 1, g * LANES:(g + 1) * LANES] * win
        acc = term if acc is None else acc + term
    return acc


def _mixer_kernel(x_ref, pre_g_ref, w_in_ref, ca_w_ref, cb_w_ref, cb_b_ref, ln_g_ref, ln_b_ref,
                  w_out_ref, post_g_ref, o_ref, h_scr, b_scr, cb_scr, y_scr, proj_scr, cv_buf, z_buf):
    tm = x_ref.shape[0]
    d_a = ca_w_ref.shape[1]
    n_groups = d_a // LANES

    @pl.when(pl.program_id(0) == 0)
    def _():
        for g in range(n_groups):
            _store_rows(cv_buf, g, 0, jnp.zeros((HALO_A, LANES), F32))
            _store_rows(z_buf, g, 0, jnp.zeros((HALO_B, LANES), F32))

    def pre_norm(rows):
        h_scr[rows, :] = _rmsnorm(x_ref[rows, :], pre_g_ref[...]).astype(BF16)

    def out_proj(rows):
        proj_scr[rows, :] = jnp.dot(y_scr[rows, :], w_out_ref[...], preferred_element_type=F32)

    def post_norm(rows):
        o_ref[rows, :] = x_ref[rows, :] + _rmsnorm(proj_scr[rows, :], post_g_ref[...])

    def mix(rows, extra):
        base = rows.start

        def in_proj(*col_starts, width=LANES):
            w = jnp.concatenate([w_in_ref[:, c:c + width] for c in col_starts], axis=-1)
            return jnp.dot(h_scr[rows, :], w, preferred_element_type=F32)

        def stage_glu(g):
            glu = in_proj(3 * d_a + g * LANES, 4 * d_a + g * LANES)
            _store_rows(z_buf, g, HALO_B + base, glu[:, :LANES] * jax.nn.sigmoid(glu[:, LANES:]))

        def stage_cv(g0):
            cv = in_proj(d_a + g0 * LANES, 2 * d_a + g0 * LANES, width=2 * LANES)
            for g in (g0, g0 + 1):
                lo = (g - g0) * LANES
                _store_rows(cv_buf, g, HALO_A + base, cv[:, lo:lo + LANES] * cv[:, 2 * LANES + lo:3 * LANES + lo])

        def stage_b():
            b_scr[rows, :] = in_proj(0, width=d_a)

        after_group = {0: [lambda: stage_glu(1), lambda: stage_cv(0)],
                       1: [lambda: stage_glu(2), lambda: stage_cv(2)],
                       2: [lambda: stage_glu(3), stage_b],
                       3: []}

        stage_glu(0)
        for g in range(n_groups):
            for r0 in range(base, base + SUB_ROWS, CONV_ROWS):
                cb_scr[r0:r0 + CONV_ROWS, g * LANES:(g + 1) * LANES] = _causal_dwconv(z_buf, cb_w_ref, HALO_B, g, r0)
            for stage in after_group[g] + extra.get(g, []):
                stage()

        for g in range(n_groups):
            for r0 in range(base, base + SUB_ROWS, CONV_ROWS):
                rr, lanes = slice(r0, r0 + CONV_ROWS), slice(g * LANES, (g + 1) * LANES)
                y_scr[rr, lanes] = (b_scr[rr, lanes] * _causal_dwconv(cv_buf, ca_w_ref, HALO_A, g, r0)).astype(BF16)

        cb = cb_scr[rows, :] + cb_b_ref[...]
        mu = jnp.mean(cb, axis=-1, keepdims=True)
        cen = cb - mu
        var = jnp.mean(cen * cen, axis=-1, keepdims=True)
        ln = cen * lax.rsqrt(var + LN_EPS) * ln_g_ref[...] + ln_b_ref[...]
        y_scr[rows, d_a:] = (ln * jax.nn.sigmoid(ln)).astype(BF16)

    for rows in _halves(tm):
        pre_norm(rows)
        mix(rows, {})
        out_proj(rows)
        post_norm(rows)

    for buf, halo in ((cv_buf, HALO_A), (z_buf, HALO_B)):
        for g in range(n_groups):
            _store_rows(buf, g, 0, _load_rows(buf, g, tm, halo))


def _kv_kernel(mem_ref, g_ref, w_k_ref, w_v_ref, kt_ref, v_ref):
    mem_n = _rmsnorm(mem_ref[...], g_ref[...]).astype(BF16)
    k = jnp.dot(mem_n, w_k_ref[...].astype(BF16), preferred_element_type=F32)
    v = jnp.dot(mem_n, w_v_ref[...].astype(BF16), preferred_element_type=F32)
    kt_ref[...] = k.T.astype(BF16)
    v_ref[...] = v.astype(BF16)


def _xattn_kernel(x_ref, pre_g_ref, w_q_ref, kt_ref, v_ref, w_o_ref, post_g_ref, o_ref,
                  h_scr, q_scr, o_scr, proj_scr):
    hd = kt_ref.shape[0] // XA_HEADS

    def pre_norm(rows):
        h_scr[rows, :] = _rmsnorm(x_ref[rows, :], pre_g_ref[...]).astype(BF16)

    def q_proj(rows):
        q = jnp.dot(h_scr[rows, :], w_q_ref[...], preferred_element_type=F32)
        q_scr[rows, :] = (q * (hd ** -0.5)).astype(BF16)

    def heads(rows):
        for i in range(XA_HEADS):
            sl = slice(i * hd, (i + 1) * hd)
            s = jnp.dot(q_scr[rows, sl], kt_ref[sl, :], preferred_element_type=F32)
            e = jnp.exp(s - jnp.max(s, axis=-1, keepdims=True))
            p = e * (1.0 / jnp.sum(e, axis=-1, keepdims=True))
            o_scr[rows, sl] = jnp.dot(p.astype(BF16), v_ref[:, sl], preferred_element_type=F32).astype(BF16)

    def out_proj(rows):
        proj_scr[rows, :] = jnp.dot(o_scr[rows, :], w_o_ref[...], preferred_element_type=F32)

    def post_norm(rows):
        o_ref[rows, :] = x_ref[rows, :] + _rmsnorm(proj_scr[rows, :], post_g_ref[...])

    first, second = _halves(x_ref.shape[0])
    pre_norm(first)
    q_proj(first)
    pre_norm(second)
    heads(first)
    q_proj(second)
    out_proj(first)
    heads(second)
    post_norm(first)
    out_proj(second)
    post_norm(second)


def _ffn_kernel(x_ref, pre_g_ref, w_gate_ref, w_up_ref, w_down_ref, post_g_ref, o_ref,
                h_scr, a_scr, proj_scr):
    d_ff = w_gate_ref.shape[1]

    def pre_norm(rows):
        h_scr[rows, :] = _rmsnorm(x_ref[rows, :], pre_g_ref[...]).astype(BF16)

    def gate_up(rows):
        for c0 in range(0, d_ff, FFN_COLS):
            cols = slice(c0, min(c0 + FFN_COLS, d_ff))
            g = jnp.dot(h_scr[rows, :], w_gate_ref[:, cols], preferred_element_type=F32)
            u = jnp.dot(h_scr[rows, :], w_up_ref[:, cols], preferred_element_type=F32)
            a_scr[rows, cols] = (g * jax.nn.sigmoid(g) * u).astype(BF16)

    def down_proj(rows):
        proj_scr[rows, :] = jnp.dot(a_scr[rows, :], w_down_ref[...], preferred_element_type=F32)

    def post_norm(rows):
        o_ref[rows, :] = x_ref[rows, :] + _rmsnorm(proj_scr[rows, :], post_g_ref[...])

    first, second = _halves(x_ref.shape[0])
    pre_norm(first)
    gate_up(first)
    pre_norm(second)
    down_proj(first)
    gate_up(second)
    post_norm(first)
    down_proj(second)
    post_norm(second)


def _resident(shape):
    return pl.BlockSpec(shape, lambda i: (0,) * len(shape), pipeline_mode=pl.Buffered(1))


def _row_tiled_call(body, x, consts, scratch_shapes, name, row_tile):
    s, d = x.shape
    assert s % row_tile == 0 and row_tile % SUB_ROWS == 0
    row_spec = pl.BlockSpec((row_tile, d), lambda i: (i, 0))
    return pl.pallas_call(
        body,
        grid=(s // row_tile,),
        in_specs=[row_spec] + [_resident(c.shape) for c in consts],
        out_specs=row_spec,
        out_shape=jax.ShapeDtypeStruct((s, d), x.dtype),
        scratch_shapes=scratch_shapes,
        compiler_params=pltpu.CompilerParams(
            dimension_semantics=("arbitrary",), vmem_limit_bytes=VMEM_LIMIT_BYTES),
        name=name,
    )(x, *consts)


def _row(v):
    return v.reshape(1, -1).astype(F32)


def kernel(x, mem, mix_pre_g, w_mix_in, conv_a_w, conv_b_w, conv_b_b, ln_b_g, ln_b_b, w_mix_out,
           mix_post_g, xa_pre_g, mem_norm_g, w_q, w_k, w_v, w_o, xa_post_g, ffn_pre_g, w_gate, w_up,
           w_down, ffn_post_g):
    bsz, seq, d = x.shape
    assert bsz == 1 and mem.shape[0] == 1
    d_a = conv_a_w.shape[1]
    d_b = conv_b_w.shape[1]
    assert d_a == d_b and w_mix_in.shape[1] == 3 * d_a + 2 * d_b and d_a % (2 * LANES) == 0
    assert conv_a_w.shape[0] == CONV_A_W and conv_b_w.shape[0] == CONV_B_W
    x2 = x.reshape(seq, d)

    x2 = _row_tiled_call(
        _mixer_kernel, x2,
        [_row(mix_pre_g), w_mix_in.astype(BF16), conv_a_w, conv_b_w, _row(conv_b_b), _row(ln_b_g),
         _row(ln_b_b), w_mix_out.astype(BF16), _row(mix_post_g)],
        [pltpu.VMEM((MIXER_ROW_TILE, d), BF16),
         pltpu.VMEM((MIXER_ROW_TILE, d_a), F32),
         pltpu.VMEM((MIXER_ROW_TILE, d_b), F32),
         pltpu.VMEM((MIXER_ROW_TILE, d_a + d_b), BF16),
         pltpu.VMEM((MIXER_ROW_TILE, d), F32),
         pltpu.VMEM((d_a // LANES, ROW_PITCH * (HALO_A + MIXER_ROW_TILE), LANES), F32),
         pltpu.VMEM((d_b // LANES, ROW_PITCH * (HALO_B + MIXER_ROW_TILE), LANES), F32)],
        "mixer", MIXER_ROW_TILE)

    n_mem = mem.shape[1]
    e = w_k.shape[1]
    kt, v = pl.pallas_call(
        _kv_kernel,
        out_shape=(jax.ShapeDtypeStruct((e, n_mem), BF16), jax.ShapeDtypeStruct((n_mem, e), BF16)),
        compiler_params=pltpu.CompilerParams(vmem_limit_bytes=VMEM_LIMIT_BYTES),
        name="kv_proj",
    )(mem.reshape(n_mem, d), _row(mem_norm_g), w_k, w_v)

    x2 = _row_tiled_call(
        _xattn_kernel, x2,
        [_row(xa_pre_g), w_q.astype(BF16), kt, v, w_o.astype(BF16), _row(xa_post_g)],
        [pltpu.VMEM((ROW_TILE, d), BF16),
         pltpu.VMEM((ROW_TILE, e), BF16),
         pltpu.VMEM((ROW_TILE, e), BF16),
         pltpu.VMEM((ROW_TILE, d), F32)],
        "xattn", ROW_TILE)

    d_ff = w_gate.shape[1]
    x2 = _row_tiled_call(
        _ffn_kernel, x2,
        [_row(ffn_pre_g), w_gate.astype(BF16), w_up.astype(BF16), w_down.astype(BF16),
         _row(ffn_post_g)],
        [pltpu.VMEM((ROW_TILE, d), BF16),
         pltpu.VMEM((ROW_TILE, d_ff), BF16),
         pltpu.VMEM((ROW_TILE, d), F32)],
        "ffn", ROW_TILE)
    return x2.reshape(bsz, seq, d)
```

```python
import jax
import jax.numpy as jnp
from jax import lax
from jax.experimental import pallas as pl
from jax.experimental.pallas import tpu as pltpu

RMS_EPS = 1e-6
LN_EPS = 1e-5
CONV_A_W = 3
CONV_B_W = 31
XA_HEADS = 4

ROW_TILE = 1024
SUB_ROWS = 512
MIXER_ROW_TILE = SUB_ROWS
HALO_A = 8
HALO_B = 32
LANES = 128
ROW_PITCH = 2
CONV_ROWS = 128
FFN_COLS = 1024
WEIGHT_CAST_STEPS = 8
BF16_SUBLANES = 16
VMEM_LIMIT_BYTES = 58 * 1024 * 1024

F32 = jnp.float32
BF16 = jnp.bfloat16


def _rmsnorm(x, g):
    ms = jnp.mean(x * x, axis=-1, keepdims=True)
    return x * lax.rsqrt(ms + RMS_EPS) * g


def _halves(tm):
    return [slice(r, r + SUB_ROWS) for r in range(0, tm, SUB_ROWS)]


def _store_rows(buf, g, row0, val):
    buf[g, pl.ds(ROW_PITCH * row0, val.shape[0], stride=ROW_PITCH), :] = val


def _load_rows(buf, g, row0, rows):
    return buf[g, pl.ds(ROW_PITCH * row0, rows, stride=ROW_PITCH), :]


def _causal_dwconv(buf, w_ref, halo, g, r0):
    n_taps = w_ref.shape[0]
    acc = None
    for k in range(n_taps):
        win = _load_rows(buf, g, halo - (n_taps - 1) + k + r0, CONV_ROWS)
        term = w_ref[k:k + 1, g * LANES:(g + 1) * LANES] * win
        acc = term if acc is None else acc + term
    return acc


def _mixer_kernel(step, x_ref, pre_g_ref, w_in_ref, ca_w_ref, cb_w_ref, cb_b_ref, ln_g_ref, ln_b_ref,
                  w_out_ref, post_g_ref, o_ref, h_scr, b_scr, cb_scr, y_scr, cv_buf, z_buf):
    tm = x_ref.shape[0]
    d_a = ca_w_ref.shape[1]
    n_groups = d_a // LANES

    @pl.when(step == 0)
    def _():
        for g in range(n_groups):
            _store_rows(cv_buf, g, 0, jnp.zeros((HALO_A, LANES), F32))
            _store_rows(z_buf, g, 0, jnp.zeros((HALO_B, LANES), F32))

    h_scr[...] = _rmsnorm(x_ref[...], pre_g_ref[...]).astype(BF16)

    def in_proj(*col_starts, width=LANES):
        w = jnp.concatenate([w_in_ref[:, c:c + width] for c in col_starts], axis=-1)
        return jnp.dot(h_scr[...], w, preferred_element_type=F32)

    def stage_glu(g):
        glu = in_proj(3 * d_a + g * LANES, 4 * d_a + g * LANES)
        _store_rows(z_buf, g, HALO_B, glu[:, :LANES] * jax.nn.sigmoid(glu[:, LANES:]))

    def stage_cv(g0):
        cv = in_proj(d_a + g0 * LANES, 2 * d_a + g0 * LANES, width=2 * LANES)
        for g in (g0, g0 + 1):
            lo = (g - g0) * LANES
            _store_rows(cv_buf, g, HALO_A, cv[:, lo:lo + LANES] * cv[:, 2 * LANES + lo:3 * LANES + lo])

    def stage_b():
        b_scr[...] = in_proj(0, width=d_a)

    after_group = {0: [lambda: stage_glu(1), lambda: stage_cv(0)],
                   1: [lambda: stage_glu(2), lambda: stage_cv(2)],
                   2: [lambda: stage_glu(3), stage_b],
                   3: []}

    stage_glu(0)
    for g in range(n_groups):
        for r0 in range(0, tm, CONV_ROWS):
            cb_scr[r0:r0 + CONV_ROWS, g * LANES:(g + 1) * LANES] = _causal_dwconv(z_buf, cb_w_ref, HALO_B, g, r0)
        for stage in after_group[g]:
            stage()

    for g in range(n_groups):
        for r0 in range(0, tm, CONV_ROWS):
            rows, lanes = slice(r0, r0 + CONV_ROWS), slice(g * LANES, (g + 1) * LANES)
            y_scr[rows, lanes] = (b_scr[rows, lanes] * _causal_dwconv(cv_buf, ca_w_ref, HALO_A, g, r0)).astype(BF16)

    cb = cb_scr[...] + cb_b_ref[...]
    mu = jnp.mean(cb, axis=-1, keepdims=True)
    cen = cb - mu
    var = jnp.mean(cen * cen, axis=-1, keepdims=True)
    ln = cen * lax.rsqrt(var + LN_EPS) * ln_g_ref[...] + ln_b_ref[...]
    y_scr[:, d_a:] = (ln * jax.nn.sigmoid(ln)).astype(BF16)

    proj = jnp.dot(y_scr[...], w_out_ref[...], preferred_element_type=F32)
    o_ref[...] = x_ref[...] + _rmsnorm(proj, post_g_ref[...])

    for buf, halo in ((cv_buf, HALO_A), (z_buf, HALO_B)):
        for g in range(n_groups):
            _store_rows(buf, g, 0, _load_rows(buf, g, tm, halo))


def _kv_kernel(mem_ref, g_ref, w_k_ref, w_v_ref, kt_ref, v_ref):
    mem_n = _rmsnorm(mem_ref[...], g_ref[...]).astype(BF16)
    k = jnp.dot(mem_n, w_k_ref[...].astype(BF16), preferred_element_type=F32)
    v = jnp.dot(mem_n, w_v_ref[...].astype(BF16), preferred_element_type=F32)
    kt_ref[...] = k.T.astype(BF16)
    v_ref[...] = v.astype(BF16)


def _xattn_kernel(step, x_ref, pre_g_ref, w_q_ref, kt_ref, v_ref, w_o_ref, post_g_ref, o_ref,
                  h_scr, q_scr, o_scr, proj_scr):
    del step
    hd = kt_ref.shape[0] // XA_HEADS

    def pre_norm(rows):
        h_scr[rows, :] = _rmsnorm(x_ref[rows, :], pre_g_ref[...]).astype(BF16)

    def q_proj(rows):
        q = jnp.dot(h_scr[rows, :], w_q_ref[...], preferred_element_type=F32)
        q_scr[rows, :] = (q * (hd ** -0.5)).astype(BF16)

    def heads(rows):
        for i in range(XA_HEADS):
            sl = slice(i * hd, (i + 1) * hd)
            s = jnp.dot(q_scr[rows, sl], kt_ref[sl, :], preferred_element_type=F32)
            e = jnp.exp(s - jnp.max(s, axis=-1, keepdims=True))
            p = e * (1.0 / jnp.sum(e, axis=-1, keepdims=True))
            o_scr[rows, sl] = jnp.dot(p.astype(BF16), v_ref[:, sl], preferred_element_type=F32).astype(BF16)

    def out_proj(rows):
        proj_scr[rows, :] = jnp.dot(o_scr[rows, :], w_o_ref[...], preferred_element_type=F32)

    def post_norm(rows):
        o_ref[rows, :] = x_ref[rows, :] + _rmsnorm(proj_scr[rows, :], post_g_ref[...])

    first, second = _halves(x_ref.shape[0])
    pre_norm(first)
    q_proj(first)
    pre_norm(second)
    heads(first)
    q_proj(second)
    out_proj(first)
    heads(second)
    post_norm(first)
    out_proj(second)
    post_norm(second)


def _ffn_kernel(step, x_ref, pre_g_ref, w_gate_ref, w_up_ref, w_down_ref, post_g_ref, o_ref,
                h_scr, a_scr, proj_scr):
    del step
    d_ff = w_gate_ref.shape[1]

    def pre_norm(rows):
        h_scr[rows, :] = _rmsnorm(x_ref[rows, :], pre_g_ref[...]).astype(BF16)

    def gate_up(rows):
        for c0 in range(0, d_ff, FFN_COLS):
            cols = slice(c0, min(c0 + FFN_COLS, d_ff))
            g = jnp.dot(h_scr[rows, :], w_gate_ref[:, cols], preferred_element_type=F32)
            u = jnp.dot(h_scr[rows, :], w_up_ref[:, cols], preferred_element_type=F32)
            a_scr[rows, cols] = (g * jax.nn.sigmoid(g) * u).astype(BF16)

    def down_proj(rows):
        proj_scr[rows, :] = jnp.dot(a_scr[rows, :], w_down_ref[...], preferred_element_type=F32)

    def post_norm(rows):
        o_ref[rows, :] = x_ref[rows, :] + _rmsnorm(proj_scr[rows, :], post_g_ref[...])

    first, second = _halves(x_ref.shape[0])
    pre_norm(first)
    gate_up(first)
    pre_norm(second)
    down_proj(first)
    gate_up(second)
    post_norm(first)
    down_proj(second)
    post_norm(second)


def _resident(shape):
    return pl.BlockSpec(shape, lambda i: (0,) * len(shape), pipeline_mode=pl.Buffered(1))


class _Bf16Weight:
    def __init__(self, array):
        assert array.ndim == 2 and array.shape[0] % (WEIGHT_CAST_STEPS * BF16_SUBLANES) == 0
        self.array = array
        self.chunk_rows = array.shape[0] // WEIGHT_CAST_STEPS


def _row_tiled_call(body, x, consts, scratch_shapes, name, row_tile):
    s, d = x.shape
    assert s % row_tile == 0 and row_tile % SUB_ROWS == 0
    n_tiles = s // row_tile
    is_weight = [isinstance(c, _Bf16Weight) for c in consts]
    weights = [c for c in consts if isinstance(c, _Bf16Weight)]
    row_spec = pl.BlockSpec((row_tile, d), lambda i: (jnp.maximum(i - WEIGHT_CAST_STEPS, 0), 0))

    def const_spec(c):
        if isinstance(c, _Bf16Weight):
            return pl.BlockSpec((c.chunk_rows, c.array.shape[1]),
                                lambda i: (jnp.minimum(i, WEIGHT_CAST_STEPS - 1), 0))
        return _resident(c.shape)

    def call_body(x_ref, *refs):
        const_refs = refs[:len(consts)]
        o_ref = refs[len(consts)]
        weight_scr = refs[len(consts) + 1:len(consts) + 1 + len(weights)]
        scratch = refs[len(consts) + 1 + len(weights):]
        chunk_refs = [r for r, w in zip(const_refs, is_weight) if w]
        i = pl.program_id(0)

        @pl.when(i < WEIGHT_CAST_STEPS)
        def _():
            for c, chunk_ref, w_scr in zip(weights, chunk_refs, weight_scr):
                row0 = pl.multiple_of(i * c.chunk_rows, c.chunk_rows)
                w_scr[pl.ds(row0, c.chunk_rows), :] = chunk_ref[...].astype(BF16)

        @pl.when(i >= WEIGHT_CAST_STEPS)
        def _():
            cast = iter(weight_scr)
            body_consts = [next(cast) if w else r for r, w in zip(const_refs, is_weight)]
            body(i - WEIGHT_CAST_STEPS, x_ref, *body_consts, o_ref, *scratch)

    return pl.pallas_call(
        call_body,
        grid=(WEIGHT_CAST_STEPS + n_tiles,),
        in_specs=[row_spec] + [const_spec(c) for c in consts],
        out_specs=row_spec,
        out_shape=jax.ShapeDtypeStruct((s, d), x.dtype),
        scratch_shapes=[pltpu.VMEM(c.array.shape, BF16) for c in weights] + scratch_shapes,
        compiler_params=pltpu.CompilerParams(
            dimension_semantics=("arbitrary",), vmem_limit_bytes=VMEM_LIMIT_BYTES),
        name=name,
    )(x, *[c.array if isinstance(c, _Bf16Weight) else c for c in consts])


def _row(v):
    return v.reshape(1, -1).astype(F32)


def kernel(x, mem, mix_pre_g, w_mix_in, conv_a_w, conv_b_w, conv_b_b, ln_b_g, ln_b_b, w_mix_out,
           mix_post_g, xa_pre_g, mem_norm_g, w_q, w_k, w_v, w_o, xa_post_g, ffn_pre_g, w_gate, w_up,
           w_down, ffn_post_g):
    bsz, seq, d = x.shape
    assert bsz == 1 and mem.shape[0] == 1
    d_a = conv_a_w.shape[1]
    d_b = conv_b_w.shape[1]
    assert d_a == d_b and w_mix_in.shape[1] == 3 * d_a + 2 * d_b and d_a % (2 * LANES) == 0
    assert conv_a_w.shape[0] == CONV_A_W and conv_b_w.shape[0] == CONV_B_W
    x2 = x.reshape(seq, d)

    x2 = _row_tiled_call(
        _mixer_kernel, x2,
        [_row(mix_pre_g), _Bf16Weight(w_mix_in), conv_a_w, conv_b_w, _row(conv_b_b), _row(ln_b_g),
         _row(ln_b_b), _Bf16Weight(w_mix_out), _row(mix_post_g)],
        [pltpu.VMEM((MIXER_ROW_TILE, d), BF16),
         pltpu.VMEM((MIXER_ROW_TILE, d_a), F32),
         pltpu.VMEM((MIXER_ROW_TILE, d_b), F32),
         pltpu.VMEM((MIXER_ROW_TILE, d_a + d_b), BF16),
         pltpu.VMEM((d_a // LANES, ROW_PITCH * (HALO_A + MIXER_ROW_TILE), LANES), F32),
         pltpu.VMEM((d_b // LANES, ROW_PITCH * (HALO_B + MIXER_ROW_TILE), LANES), F32)],
        "mixer", MIXER_ROW_TILE)

    n_mem = mem.shape[1]
    e = w_k.shape[1]
    kt, v = pl.pallas_call(
        _kv_kernel,
        out_shape=(jax.ShapeDtypeStruct((e, n_mem), BF16), jax.ShapeDtypeStruct((n_mem, e), BF16)),
        compiler_params=pltpu.CompilerParams(vmem_limit_bytes=VMEM_LIMIT_BYTES),
        name="kv_proj",
    )(mem.reshape(n_mem, d), _row(mem_norm_g), w_k, w_v)

    x2 = _row_tiled_call(
        _xattn_kernel, x2,
        [_row(xa_pre_g), _Bf16Weight(w_q), kt, v, _Bf16Weight(w_o), _row(xa_post_g)],
        [pltpu.VMEM((ROW_TILE, d), BF16),
         pltpu.VMEM((ROW_TILE, e), BF16),
         pltpu.VMEM((ROW_TILE, e), BF16),
         pltpu.VMEM((ROW_TILE, d), F32)],
        "xattn", ROW_TILE)

    d_ff = w_gate.shape[1]
    x2 = _row_tiled_call(
        _ffn_kernel, x2,
        [_row(ffn_pre_g), _Bf16Weight(w_gate), _Bf16Weight(w_up), _Bf16Weight(w_down),
         _row(ffn_post_g)],
        [pltpu.VMEM((ROW_TILE, d), BF16),
         pltpu.VMEM((ROW_TILE, d_ff), BF16),
         pltpu.VMEM((ROW_TILE, d), F32)],
        "ffn", ROW_TILE)
    return x2.reshape(bsz, seq, d)
```

```python
import jax
import jax.numpy as jnp
from jax import lax
from jax.experimental import pallas as pl
from jax.experimental.pallas import tpu as pltpu

RMS_EPS = 1e-6
LN_EPS = 1e-5
CONV_A_W = 3
CONV_B_W = 31
XA_HEADS = 4

ROW_TILE = 1024
SUB_ROWS = 512
MIXER_ROW_TILE = SUB_ROWS
HALO_A = 8
HALO_B = 32
LANES = 128
ROW_PITCH = 3
CONV_ROWS = 128
FFN_COLS = 1024
WEIGHT_CAST_STEPS = 8
BF16_SUBLANES = 16
VMEM_LIMIT_BYTES = 58 * 1024 * 1024

F32 = jnp.float32
BF16 = jnp.bfloat16


def _rmsnorm(x, g):
    ms = jnp.mean(x * x, axis=-1, keepdims=True)
    return x * lax.rsqrt(ms + RMS_EPS) * g


def _halves(tm):
    return [slice(r, r + SUB_ROWS) for r in range(0, tm, SUB_ROWS)]


def _store_rows(buf, g, row0, val):
    buf[g, pl.ds(ROW_PITCH * row0, val.shape[0], stride=ROW_PITCH), :] = val


def _load_rows(buf, g, row0, rows):
    return buf[g, pl.ds(ROW_PITCH * row0, rows, stride=ROW_PITCH), :]


def _causal_dwconv(buf, w_ref, halo, g, r0):
    n_taps = w_ref.shape[0]
    acc = None
    for k in range(n_taps):
        win = _load_rows(buf, g, halo - (n_taps - 1) + k + r0, CONV_ROWS)
        term = w_ref[k:k + 1, g * LANES:(g + 1) * LANES] * win
        acc = term if acc is None else acc + term
    return acc


def _mixer_kernel(step, x_ref, pre_g_ref, w_in_ref, ca_w_ref, cb_w_ref, cb_b_ref, ln_g_ref, ln_b_ref,
                  w_out_ref, post_g_ref, o_ref, h_scr, b_scr, cb_scr, y_scr, cv_buf, z_buf):
    tm = x_ref.shape[0]
    d_a = ca_w_ref.shape[1]
    n_groups = d_a // LANES

    @pl.when(step == 0)
    def _():
        for g in range(n_groups):
            _store_rows(cv_buf, g, 0, jnp.zeros((HALO_A, LANES), F32))
            _store_rows(z_buf, g, 0, jnp.zeros((HALO_B, LANES), F32))

    h_scr[...] = _rmsnorm(x_ref[...], pre_g_ref[...]).astype(BF16)

    def in_proj(*col_starts, width=LANES):
        w = jnp.concatenate([w_in_ref[:, c:c + width] for c in col_starts], axis=-1)
        return jnp.dot(h_scr[...], w, preferred_element_type=F32)

    def stage_glu(g):
        glu = in_proj(3 * d_a + g * LANES, 4 * d_a + g * LANES)
        _store_rows(z_buf, g, HALO_B, glu[:, :LANES] * jax.nn.sigmoid(glu[:, LANES:]))

    def stage_cv(g0):
        cv = in_proj(d_a + g0 * LANES, 2 * d_a + g0 * LANES, width=2 * LANES)
        for g in (g0, g0 + 1):
            lo = (g - g0) * LANES
            _store_rows(cv_buf, g, HALO_A, cv[:, lo:lo + LANES] * cv[:, 2 * LANES + lo:3 * LANES + lo])

    def stage_b():
        b_scr[...] = in_proj(0, width=d_a)

    after_group = {0: [lambda: stage_glu(1), lambda: stage_cv(0)],
                   1: [lambda: stage_glu(2), lambda: stage_cv(2)],
                   2: [lambda: stage_glu(3), stage_b],
                   3: []}

    stage_glu(0)
    for g in range(n_groups):
        for r0 in range(0, tm, CONV_ROWS):
            cb_scr[r0:r0 + CONV_ROWS, g * LANES:(g + 1) * LANES] = _causal_dwconv(z_buf, cb_w_ref, HALO_B, g, r0)
        for stage in after_group[g]:
            stage()

    for g in range(n_groups):
        for r0 in range(0, tm, CONV_ROWS):
            rows, lanes = slice(r0, r0 + CONV_ROWS), slice(g * LANES, (g + 1) * LANES)
            y_scr[rows, lanes] = (b_scr[rows, lanes] * _causal_dwconv(cv_buf, ca_w_ref, HALO_A, g, r0)).astype(BF16)

    cb = cb_scr[...] + cb_b_ref[...]
    mu = jnp.mean(cb, axis=-1, keepdims=True)
    cen = cb - mu
    var = jnp.mean(cen * cen, axis=-1, keepdims=True)
    ln = cen * lax.rsqrt(var + LN_EPS) * ln_g_ref[...] + ln_b_ref[...]
    y_scr[:, d_a:] = (ln * jax.nn.sigmoid(ln)).astype(BF16)

    proj = jnp.dot(y_scr[...], w_out_ref[...], preferred_element_type=F32)
    o_ref[...] = x_ref[...] + _rmsnorm(proj, post_g_ref[...])

    for buf, halo in ((cv_buf, HALO_A), (z_buf, HALO_B)):
        for g in range(n_groups):
            _store_rows(buf, g, 0, _load_rows(buf, g, tm, halo))


def _kv_kernel(mem_ref, g_ref, w_k_ref, w_v_ref, kt_ref, v_ref):
    mem_n = _rmsnorm(mem_ref[...], g_ref[...]).astype(BF16)
    k = jnp.dot(mem_n, w_k_ref[...].astype(BF16), preferred_element_type=F32)
    v = jnp.dot(mem_n, w_v_ref[...].astype(BF16), preferred_element_type=F32)
    kt_ref[...] = k.T.astype(BF16)
    v_ref[...] = v.astype(BF16)


def _xattn_kernel(step, x_ref, pre_g_ref, w_q_ref, kt_ref, v_ref, w_o_ref, post_g_ref, o_ref,
                  h_scr, q_scr, o_scr, proj_scr):
    del step
    hd = kt_ref.shape[0] // XA_HEADS

    def pre_norm(rows):
        h_scr[rows, :] = _rmsnorm(x_ref[rows, :], pre_g_ref[...]).astype(BF16)

    def q_proj(rows):
        q = jnp.dot(h_scr[rows, :], w_q_ref[...], preferred_element_type=F32)
        q_scr[rows, :] = (q * (hd ** -0.5)).astype(BF16)

    def heads(rows):
        for i in range(XA_HEADS):
            sl = slice(i * hd, (i + 1) * hd)
            s = jnp.dot(q_scr[rows, sl], kt_ref[sl, :], preferred_element_type=F32)
            e = jnp.exp(s - jnp.max(s, axis=-1, keepdims=True))
            p = e * (1.0 / jnp.sum(e, axis=-1, keepdims=True))
            o_scr[rows, sl] = jnp.dot(p.astype(BF16), v_ref[:, sl], preferred_element_type=F32).astype(BF16)

    def out_proj(rows):
        proj_scr[rows, :] = jnp.dot(o_scr[rows, :], w_o_ref[...], preferred_element_type=F32)

    def post_norm(rows):
        o_ref[rows, :] = x_ref[rows, :] + _rmsnorm(proj_scr[rows, :], post_g_ref[...])

    first, second = _halves(x_ref.shape[0])
    pre_norm(first)
    q_proj(first)
    pre_norm(second)
    heads(first)
    q_proj(second)
    out_proj(first)
    heads(second)
    post_norm(first)
    out_proj(second)
    post_norm(second)


def _ffn_kernel(step, x_ref, pre_g_ref, w_gate_ref, w_up_ref, w_down_ref, post_g_ref, o_ref,
                h_scr, a_scr, proj_scr):
    del step
    d_ff = w_gate_ref.shape[1]

    def pre_norm(rows):
        h_scr[rows, :] = _rmsnorm(x_ref[rows, :], pre_g_ref[...]).astype(BF16)

    def gate_up(rows):
        for c0 in range(0, d_ff, FFN_COLS):
            cols = slice(c0, min(c0 + FFN_COLS, d_ff))
            g = jnp.dot(h_scr[rows, :], w_gate_ref[:, cols], preferred_element_type=F32)
            u = jnp.dot(h_scr[rows, :], w_up_ref[:, cols], preferred_element_type=F32)
            a_scr[rows, cols] = (g * jax.nn.sigmoid(g) * u).astype(BF16)

    def down_proj(rows):
        proj_scr[rows, :] = jnp.dot(a_scr[rows, :], w_down_ref[...], preferred_element_type=F32)

    def post_norm(rows):
        o_ref[rows, :] = x_ref[rows, :] + _rmsnorm(proj_scr[rows, :], post_g_ref[...])

    first, second = _halves(x_ref.shape[0])
    pre_norm(first)
    gate_up(first)
    pre_norm(second)
    down_proj(first)
    gate_up(second)
    post_norm(first)
    down_proj(second)
    post_norm(second)


def _resident(shape):
    return pl.BlockSpec(shape, lambda i: (0,) * len(shape), pipeline_mode=pl.Buffered(1))


class _Bf16Weight:
    def __init__(self, array):
        assert array.ndim == 2 and array.shape[0] % (WEIGHT_CAST_STEPS * BF16_SUBLANES) == 0
        self.array = array
        self.chunk_rows = array.shape[0] // WEIGHT_CAST_STEPS


def _row_tiled_call(body, x, consts, scratch_shapes, name, row_tile):
    s, d = x.shape
    assert s % row_tile == 0 and row_tile % SUB_ROWS == 0
    n_tiles = s // row_tile
    is_weight = [isinstance(c, _Bf16Weight) for c in consts]
    weights = [c for c in consts if isinstance(c, _Bf16Weight)]
    row_spec = pl.BlockSpec((row_tile, d), lambda i: (jnp.maximum(i - WEIGHT_CAST_STEPS, 0), 0))

    def const_spec(c):
        if isinstance(c, _Bf16Weight):
            return pl.BlockSpec((c.chunk_rows, c.array.shape[1]),
                                lambda i: (jnp.minimum(i, WEIGHT_CAST_STEPS - 1), 0))
        return _resident(c.shape)

    def call_body(x_ref, *refs):
        const_refs = refs[:len(consts)]
        o_ref = refs[len(consts)]
        weight_scr = refs[len(consts) + 1:len(consts) + 1 + len(weights)]
        scratch = refs[len(consts) + 1 + len(weights):]
        chunk_refs = [r for r, w in zip(const_refs, is_weight) if w]
        i = pl.program_id(0)

        @pl.when(i < WEIGHT_CAST_STEPS)
        def _():
            for c, chunk_ref, w_scr in zip(weights, chunk_refs, weight_scr):
                row0 = pl.multiple_of(i * c.chunk_rows, c.chunk_rows)
                w_scr[pl.ds(row0, c.chunk_rows), :] = chunk_ref[...].astype(BF16)

        @pl.when(i >= WEIGHT_CAST_STEPS)
        def _():
            cast = iter(weight_scr)
            body_consts = [next(cast) if w else r for r, w in zip(const_refs, is_weight)]
            body(i - WEIGHT_CAST_STEPS, x_ref, *body_consts, o_ref, *scratch)

    return pl.pallas_call(
        call_body,
        grid=(WEIGHT_CAST_STEPS + n_tiles,),
        in_specs=[row_spec] + [const_spec(c) for c in consts],
        out_specs=row_spec,
        out_shape=jax.ShapeDtypeStruct((s, d), x.dtype),
        scratch_shapes=[pltpu.VMEM(c.array.shape, BF16) for c in weights] + scratch_shapes,
        compiler_params=pltpu.CompilerParams(
            dimension_semantics=("arbitrary",), vmem_limit_bytes=VMEM_LIMIT_BYTES),
        name=name,
    )(x, *[c.array if isinstance(c, _Bf16Weight) else c for c in consts])


def _row(v):
    return v.reshape(1, -1).astype(F32)


def kernel(x, mem, mix_pre_g, w_mix_in, conv_a_w, conv_b_w, conv_b_b, ln_b_g, ln_b_b, w_mix_out,
           mix_post_g, xa_pre_g, mem_norm_g, w_q, w_k, w_v, w_o, xa_post_g, ffn_pre_g, w_gate, w_up,
           w_down, ffn_post_g):
    bsz, seq, d = x.shape
    assert bsz == 1 and mem.shape[0] == 1
    d_a = conv_a_w.shape[1]
    d_b = conv_b_w.shape[1]
    assert d_a == d_b and w_mix_in.shape[1] == 3 * d_a + 2 * d_b and d_a % (2 * LANES) == 0
    assert conv_a_w.shape[0] == CONV_A_W and conv_b_w.shape[0] == CONV_B_W
    x2 = x.reshape(seq, d)

    x2 = _row_tiled_call(
        _mixer_kernel, x2,
        [_row(mix_pre_g), _Bf16Weight(w_mix_in), conv_a_w, conv_b_w, _row(conv_b_b), _row(ln_b_g),
         _row(ln_b_b), _Bf16Weight(w_mix_out), _row(mix_post_g)],
        [pltpu.VMEM((MIXER_ROW_TILE, d), BF16),
         pltpu.VMEM((MIXER_ROW_TILE, d_a), F32),
         pltpu.VMEM((MIXER_ROW_TILE, d_b), F32),
         pltpu.VMEM((MIXER_ROW_TILE, d_a + d_b), BF16),
         pltpu.VMEM((d_a // LANES, ROW_PITCH * (HALO_A + MIXER_ROW_TILE), LANES), F32),
         pltpu.VMEM((d_b // LANES, ROW_PITCH * (HALO_B + MIXER_ROW_TILE), LANES), F32)],
        "mixer", MIXER_ROW_TILE)

    n_mem = mem.shape[1]
    e = w_k.shape[1]
    kt, v = pl.pallas_call(
        _kv_kernel,
        out_shape=(jax.ShapeDtypeStruct((e, n_mem), BF16), jax.ShapeDtypeStruct((n_mem, e), BF16)),
        compiler_params=pltpu.CompilerParams(vmem_limit_bytes=VMEM_LIMIT_BYTES),
        name="kv_proj",
    )(mem.reshape(n_mem, d), _row(mem_norm_g), w_k, w_v)

    x2 = _row_tiled_call(
        _xattn_kernel, x2,
        [_row(xa_pre_g), _Bf16Weight(w_q), kt, v, _Bf16Weight(w_o), _row(xa_post_g)],
        [pltpu.VMEM((ROW_TILE, d), BF16),
         pltpu.VMEM((ROW_TILE, e), BF16),
         pltpu.VMEM((ROW_TILE, e), BF16),
         pltpu.VMEM((ROW_TILE, d), F32)],
        "xattn", ROW_TILE)

    d_ff = w_gate.shape[1]
    x2 = _row_tiled_call(
        _ffn_kernel, x2,
        [_row(ffn_pre_g), _Bf16Weight(w_gate), _Bf16Weight(w_up), _Bf16Weight(w_down),
         _row(ffn_post_g)],
        [pltpu.VMEM((ROW_TILE, d), BF16),
         pltpu.VMEM((ROW_TILE, d_ff), BF16),
         pltpu.VMEM((ROW_TILE, d), F32)],
        "ffn", ROW_TILE)
    return x2.reshape(bsz, seq, d)
```

```python
import jax
import jax.numpy as jnp
from jax import lax
from jax.experimental import pallas as pl
from jax.experimental.pallas import tpu as pltpu

RMS_EPS = 1e-6
LN_EPS = 1e-5
CONV_A_W = 3
CONV_B_W = 31
XA_HEADS = 4

ROW_TILE = 1024
SUB_ROWS = 512
MIXER_ROW_TILE = SUB_ROWS
HALO_A = 8
HALO_B = 32
LANES = 128
ROW_PITCH = 2
CONV_ROWS = 128
FFN_COLS = 1024
WEIGHT_CAST_STEPS = 8
BF16_SUBLANES = 16
VMEM_LIMIT_BYTES = 58 * 1024 * 1024

F32 = jnp.float32
BF16 = jnp.bfloat16


def _rmsnorm(x, g):
    ms = jnp.mean(x * x, axis=-1, keepdims=True)
    return x * lax.rsqrt(ms + RMS_EPS) * g


def _halves(tm):
    return [slice(r, r + SUB_ROWS) for r in range(0, tm, SUB_ROWS)]


def _store_rows(buf, g, row0, val):
    buf[g, pl.ds(ROW_PITCH * row0, val.shape[0], stride=ROW_PITCH), :] = val


def _load_rows(buf, g, row0, rows):
    return buf[g, pl.ds(ROW_PITCH * row0, rows, stride=ROW_PITCH), :]


def _causal_dwconv(buf, w_ref, halo, g, r0):
    n_taps = w_ref.shape[0]
    acc = None
    for k in range(n_taps):
        win = _load_rows(buf, g, halo - (n_taps - 1) + k + r0, CONV_ROWS)
        term = w_ref[k:k + 1, g * LANES:(g + 1) * LANES] * win
        acc = term if acc is None else acc + term
    return acc


def _mixer_kernel(step, x_ref, pre_g_ref, w_in_ref, ca_w_ref, cb_w_ref, cb_b_ref, ln_g_ref, ln_b_ref,
                  w_out_ref, post_g_ref, o_ref, h_scr, b_scr, cb_scr, y_scr, cv_buf, z_buf):
    tm = x_ref.shape[0]
    d_a = ca_w_ref.shape[1]
    n_groups = d_a // LANES

    @pl.when(step == 0)
    def _():
        for g in range(n_groups):
            _store_rows(cv_buf, g, 0, jnp.zeros((HALO_A, LANES), F32))
            _store_rows(z_buf, g, 0, jnp.zeros((HALO_B, LANES), F32))

    h_scr[...] = _rmsnorm(x_ref[...], pre_g_ref[...]).astype(BF16)

    def in_proj(*col_starts, width=LANES):
        w = jnp.concatenate([w_in_ref[:, c:c + width] for c in col_starts], axis=-1)
        return jnp.dot(h_scr[...], w, preferred_element_type=F32)

    def stage_glu(g):
        glu = in_proj(3 * d_a + g * LANES, 4 * d_a + g * LANES)
        _store_rows(z_buf, g, HALO_B, glu[:, :LANES] * jax.nn.sigmoid(glu[:, LANES:]))

    def stage_cv(g0):
        cv = in_proj(d_a + g0 * LANES, 2 * d_a + g0 * LANES, width=2 * LANES)
        for g in (g0, g0 + 1):
            lo = (g - g0) * LANES
            _store_rows(cv_buf, g, HALO_A, cv[:, lo:lo + LANES] * cv[:, 2 * LANES + lo:3 * LANES + lo])

    def stage_b():
        b_scr[...] = in_proj(0, width=d_a)

    after_group = {0: [lambda: stage_glu(1), lambda: stage_cv(0)],
                   1: [lambda: stage_glu(2), lambda: stage_cv(2)],
                   2: [lambda: stage_glu(3), stage_b],
                   3: []}

    stage_glu(0)
    for g in range(n_groups):
        for r0 in range(0, tm, CONV_ROWS):
            cb_scr[r0:r0 + CONV_ROWS, g * LANES:(g + 1) * LANES] = _causal_dwconv(z_buf, cb_w_ref, HALO_B, g, r0)
        for stage in after_group[g]:
            stage()

    for g in range(n_groups):
        for r0 in range(0, tm, CONV_ROWS):
            rows, lanes = slice(r0, r0 + CONV_ROWS), slice(g * LANES, (g + 1) * LANES)
            y_scr[rows, lanes] = (b_scr[rows, lanes] * _causal_dwconv(cv_buf, ca_w_ref, HALO_A, g, r0)).astype(BF16)

    cb = cb_scr[...] + cb_b_ref[...]
    mu = jnp.mean(cb, axis=-1, keepdims=True)
    cen = cb - mu
    var = jnp.mean(cen * cen, axis=-1, keepdims=True)
    ln = cen * lax.rsqrt(var + LN_EPS) * ln_g_ref[...] + ln_b_ref[...]
    y_scr[:, d_a:] = (ln * jax.nn.sigmoid(ln)).astype(BF16)

    proj = jnp.dot(y_scr[...], w_out_ref[...], preferred_element_type=F32)
    o_ref[...] = x_ref[...] + _rmsnorm(proj, post_g_ref[...])

    for buf, halo in ((cv_buf, HALO_A), (z_buf, HALO_B)):
        for g in range(n_groups):
            _store_rows(buf, g, 0, _load_rows(buf, g, tm, halo))


def _xattn_kernel(step, x_ref, pre_g_ref, w_q_ref, mem_ref, mem_g_ref, w_k_ref, w_v_ref, w_o_ref,
                  post_g_ref, o_ref, kt_ref, v_ref, h_scr, q_scr, o_scr, proj_scr):
    hd = kt_ref.shape[0] // XA_HEADS

    @pl.when(step == 0)
    def _():
        mem_n = _rmsnorm(mem_ref[...], mem_g_ref[...]).astype(BF16)
        k = jnp.dot(mem_n, w_k_ref[...], preferred_element_type=F32)
        v = jnp.dot(mem_n, w_v_ref[...], preferred_element_type=F32)
        kt_ref[...] = k.T.astype(BF16)
        v_ref[...] = v.astype(BF16)

    def pre_norm(rows):
        h_scr[rows, :] = _rmsnorm(x_ref[rows, :], pre_g_ref[...]).astype(BF16)

    def q_proj(rows):
        q = jnp.dot(h_scr[rows, :], w_q_ref[...], preferred_element_type=F32)
        q_scr[rows, :] = (q * (hd ** -0.5)).astype(BF16)

    def heads(rows):
        for i in range(XA_HEADS):
            sl = slice(i * hd, (i + 1) * hd)
            s = jnp.dot(q_scr[rows, sl], kt_ref[sl, :], preferred_element_type=F32)
            e = jnp.exp(s - jnp.max(s, axis=-1, keepdims=True))
            p = e * (1.0 / jnp.sum(e, axis=-1, keepdims=True))
            o_scr[rows, sl] = jnp.dot(p.astype(BF16), v_ref[:, sl], preferred_element_type=F32).astype(BF16)

    def out_proj(rows):
        proj_scr[rows, :] = jnp.dot(o_scr[rows, :], w_o_ref[...], preferred_element_type=F32)

    def post_norm(rows):
        o_ref[rows, :] = x_ref[rows, :] + _rmsnorm(proj_scr[rows, :], post_g_ref[...])

    first, second = _halves(x_ref.shape[0])
    pre_norm(first)
    q_proj(first)
    pre_norm(second)
    heads(first)
    q_proj(second)
    out_proj(first)
    heads(second)
    post_norm(first)
    out_proj(second)
    post_norm(second)


def _ffn_kernel(step, x_ref, pre_g_ref, w_gate_ref, w_up_ref, w_down_ref, post_g_ref, o_ref,
                h_scr, a_scr, proj_scr):
    del step
    d_ff = w_gate_ref.shape[1]

    def pre_norm(rows):
        h_scr[rows, :] = _rmsnorm(x_ref[rows, :], pre_g_ref[...]).astype(BF16)

    def gate_up(rows):
        for c0 in range(0, d_ff, FFN_COLS):
            cols = slice(c0, min(c0 + FFN_COLS, d_ff))
            g = jnp.dot(h_scr[rows, :], w_gate_ref[:, cols], preferred_element_type=F32)
            u = jnp.dot(h_scr[rows, :], w_up_ref[:, cols], preferred_element_type=F32)
            a_scr[rows, cols] = (g * jax.nn.sigmoid(g) * u).astype(BF16)

    def down_proj(rows):
        proj_scr[rows, :] = jnp.dot(a_scr[rows, :], w_down_ref[...], preferred_element_type=F32)

    def post_norm(rows):
        o_ref[rows, :] = x_ref[rows, :] + _rmsnorm(proj_scr[rows, :], post_g_ref[...])

    first, second = _halves(x_ref.shape[0])
    pre_norm(first)
    gate_up(first)
    pre_norm(second)
    down_proj(first)
    gate_up(second)
    post_norm(first)
    down_proj(second)
    post_norm(second)


def _resident(shape):
    return pl.BlockSpec(shape, lambda i: (0,) * len(shape), pipeline_mode=pl.Buffered(1))


class _Bf16Weight:
    def __init__(self, array):
        assert array.ndim == 2 and array.shape[0] % (WEIGHT_CAST_STEPS * BF16_SUBLANES) == 0
        self.array = array
        self.chunk_rows = array.shape[0] // WEIGHT_CAST_STEPS


def _row_tiled_call(body, x, consts, scratch_shapes, name, row_tile):
    s, d = x.shape
    assert s % row_tile == 0 and row_tile % SUB_ROWS == 0
    n_tiles = s // row_tile
    is_weight = [isinstance(c, _Bf16Weight) for c in consts]
    weights = [c for c in consts if isinstance(c, _Bf16Weight)]
    row_spec = pl.BlockSpec((row_tile, d), lambda i: (jnp.maximum(i - WEIGHT_CAST_STEPS, 0), 0))

    def const_spec(c):
        if isinstance(c, _Bf16Weight):
            return pl.BlockSpec((c.chunk_rows, c.array.shape[1]),
                                lambda i: (jnp.minimum(i, WEIGHT_CAST_STEPS - 1), 0))
        return _resident(c.shape)

    def call_body(x_ref, *refs):
        const_refs = refs[:len(consts)]
        o_ref = refs[len(consts)]
        weight_scr = refs[len(consts) + 1:len(consts) + 1 + len(weights)]
        scratch = refs[len(consts) + 1 + len(weights):]
        chunk_refs = [r for r, w in zip(const_refs, is_weight) if w]
        i = pl.program_id(0)

        @pl.when(i < WEIGHT_CAST_STEPS)
        def _():
            for c, chunk_ref, w_scr in zip(weights, chunk_refs, weight_scr):
                row0 = pl.multiple_of(i * c.chunk_rows, c.chunk_rows)
                w_scr[pl.ds(row0, c.chunk_rows), :] = chunk_ref[...].astype(BF16)

        @pl.when(i >= WEIGHT_CAST_STEPS)
        def _():
            cast = iter(weight_scr)
            body_consts = [next(cast) if w else r for r, w in zip(const_refs, is_weight)]
            body(i - WEIGHT_CAST_STEPS, x_ref, *body_consts, o_ref, *scratch)

    return pl.pallas_call(
        call_body,
        grid=(WEIGHT_CAST_STEPS + n_tiles,),
        in_specs=[row_spec] + [const_spec(c) for c in consts],
        out_specs=row_spec,
        out_shape=jax.ShapeDtypeStruct((s, d), x.dtype),
        scratch_shapes=[pltpu.VMEM(c.array.shape, BF16) for c in weights] + scratch_shapes,
        compiler_params=pltpu.CompilerParams(
            dimension_semantics=("arbitrary",), vmem_limit_bytes=VMEM_LIMIT_BYTES),
        name=name,
    )(x, *[c.array if isinstance(c, _Bf16Weight) else c for c in consts])


def _row(v):
    return v.reshape(1, -1).astype(F32)


def kernel(x, mem, mix_pre_g, w_mix_in, conv_a_w, conv_b_w, conv_b_b, ln_b_g, ln_b_b, w_mix_out,
           mix_post_g, xa_pre_g, mem_norm_g, w_q, w_k, w_v, w_o, xa_post_g, ffn_pre_g, w_gate, w_up,
           w_down, ffn_post_g):
    bsz, seq, d = x.shape
    assert bsz == 1 and mem.shape[0] == 1
    d_a = conv_a_w.shape[1]
    d_b = conv_b_w.shape[1]
    assert d_a == d_b and w_mix_in.shape[1] == 3 * d_a + 2 * d_b and d_a % (2 * LANES) == 0
    assert conv_a_w.shape[0] == CONV_A_W and conv_b_w.shape[0] == CONV_B_W
    x2 = x.reshape(seq, d)

    x2 = _row_tiled_call(
        _mixer_kernel, x2,
        [_row(mix_pre_g), _Bf16Weight(w_mix_in), conv_a_w, conv_b_w, _row(conv_b_b), _row(ln_b_g),
         _row(ln_b_b), _Bf16Weight(w_mix_out), _row(mix_post_g)],
        [pltpu.VMEM((MIXER_ROW_TILE, d), BF16),
         pltpu.VMEM((MIXER_ROW_TILE, d_a), F32),
         pltpu.VMEM((MIXER_ROW_TILE, d_b), F32),
         pltpu.VMEM((MIXER_ROW_TILE, d_a + d_b), BF16),
         pltpu.VMEM((d_a // LANES, ROW_PITCH * (HALO_A + MIXER_ROW_TILE), LANES), F32),
         pltpu.VMEM((d_b // LANES, ROW_PITCH * (HALO_B + MIXER_ROW_TILE), LANES), F32)],
        "mixer", MIXER_ROW_TILE)

    n_mem = mem.shape[1]
    e = w_k.shape[1]
    x2 = _row_tiled_call(
        _xattn_kernel, x2,
        [_row(xa_pre_g), _Bf16Weight(w_q), mem.reshape(n_mem, d), _row(mem_norm_g), _Bf16Weight(w_k),
         _Bf16Weight(w_v), _Bf16Weight(w_o), _row(xa_post_g)],
        [pltpu.VMEM((e, n_mem), BF16),
         pltpu.VMEM((n_mem, e), BF16),
         pltpu.VMEM((ROW_TILE, d), BF16),
         pltpu.VMEM((ROW_TILE, e), BF16),
         pltpu.VMEM((ROW_TILE, e), BF16),
         pltpu.VMEM((ROW_TILE, d), F32)],
        "xattn", ROW_TILE)

    d_ff = w_gate.shape[1]
    x2 = _row_tiled_call(
        _ffn_kernel, x2,
        [_row(ffn_pre_g), _Bf16Weight(w_gate), _Bf16Weight(w_up), _Bf16Weight(w_down),
         _row(ffn_post_g)],
        [pltpu.VMEM((ROW_TILE, d), BF16),
         pltpu.VMEM((ROW_TILE, d_ff), BF16),
         pltpu.VMEM((ROW_TILE, d), F32)],
        "ffn", ROW_TILE)
    return x2.reshape(bsz, seq, d)
```

```python
import jax
import jax.numpy as jnp
from jax import lax
from jax.experimental import pallas as pl
from jax.experimental.pallas import tpu as pltpu

RMS_EPS = 1e-6
LN_EPS = 1e-5
CONV_A_W = 3
CONV_B_W = 31
XA_HEADS = 4

ROW_TILE = 1024
SUB_ROWS = 512
MIXER_ROW_TILE = SUB_ROWS
HALO_A = 8
HALO_B = 32
LANES = 128
ROW_PITCH = 2
CONV_ROWS = 128
FFN_COLS = 1024
WEIGHT_CAST_STEPS = 8
BF16_SUBLANES = 16
VMEM_LIMIT_BYTES = 58 * 1024 * 1024

F32 = jnp.float32
BF16 = jnp.bfloat16


def _rmsnorm(x, g):
    ms = jnp.mean(x * x, axis=-1, keepdims=True)
    return x * lax.rsqrt(ms + RMS_EPS) * g


def _halves(tm):
    return [slice(r, r + SUB_ROWS) for r in range(0, tm, SUB_ROWS)]


def _store_rows(buf, g, row0, val):
    buf[g, pl.ds(ROW_PITCH * row0, val.shape[0], stride=ROW_PITCH), :] = val


def _load_rows(buf, g, row0, rows):
    return buf[g, pl.ds(ROW_PITCH * row0, rows, stride=ROW_PITCH), :]


def _causal_dwconv(buf, w_ref, halo, g, r0):
    n_taps = w_ref.shape[0]
    acc = None
    for k in range(n_taps):
        win = _load_rows(buf, g, halo - (n_taps - 1) + k + r0, CONV_ROWS)
        term = w_ref[k:k + 1, g * LANES:(g + 1) * LANES] * win
        acc = term if acc is None else acc + term
    return acc


def _mixer_kernel(step, x_ref, pre_g_ref, w_in_ref, ca_w_ref, cb_w_ref, cb_b_ref, ln_g_ref, ln_b_ref,
                  w_out_ref, post_g_ref, o_ref, h_scr, b_scr, cb_scr, y_scr, cv_buf, z_buf):
    tm = x_ref.shape[0]
    d_a = ca_w_ref.shape[1]
    n_groups = d_a // LANES

    @pl.when(step == 0)
    def _():
        for g in range(n_groups):
            _store_rows(cv_buf, g, 0, jnp.zeros((HALO_A, LANES), F32))
            _store_rows(z_buf, g, 0, jnp.zeros((HALO_B, LANES), F32))

    h_scr[...] = _rmsnorm(x_ref[...], pre_g_ref[...]).astype(BF16)

    def in_proj(*col_starts, width=LANES):
        w = jnp.concatenate([w_in_ref[:, c:c + width] for c in col_starts], axis=-1)
        return jnp.dot(h_scr[...], w, preferred_element_type=F32)

    def stage_glu(g):
        glu = in_proj(3 * d_a + g * LANES, 4 * d_a + g * LANES)
        _store_rows(z_buf, g, HALO_B, glu[:, :LANES] * jax.nn.sigmoid(glu[:, LANES:]))

    def stage_cv(g0):
        cv = in_proj(d_a + g0 * LANES, 2 * d_a + g0 * LANES, width=2 * LANES)
        for g in (g0, g0 + 1):
            lo = (g - g0) * LANES
            _store_rows(cv_buf, g, HALO_A, cv[:, lo:lo + LANES] * cv[:, 2 * LANES + lo:3 * LANES + lo])

    def stage_b():
        b_scr[...] = in_proj(0, width=d_a)

    after_group = {0: [lambda: stage_glu(1), lambda: stage_cv(0)],
                   1: [lambda: stage_glu(2), lambda: stage_cv(2)],
                   2: [lambda: stage_glu(3), stage_b],
                   3: []}

    stage_glu(0)
    for g in range(n_groups):
        for r0 in range(0, tm, CONV_ROWS):
            cb_scr[r0:r0 + CONV_ROWS, g * LANES:(g + 1) * LANES] = _causal_dwconv(z_buf, cb_w_ref, HALO_B, g, r0)
        for stage in after_group[g]:
            stage()

    for g in range(n_groups):
        for r0 in range(0, tm, CONV_ROWS):
            rows, lanes = slice(r0, r0 + CONV_ROWS), slice(g * LANES, (g + 1) * LANES)
            y_scr[rows, lanes] = (b_scr[rows, lanes] * _causal_dwconv(cv_buf, ca_w_ref, HALO_A, g, r0)).astype(BF16)

    cb = cb_scr[...] + cb_b_ref[...]
    mu = jnp.mean(cb, axis=-1, keepdims=True)
    cen = cb - mu
    var = jnp.mean(cen * cen, axis=-1, keepdims=True)
    ln = cen * lax.rsqrt(var + LN_EPS) * ln_g_ref[...] + ln_b_ref[...]
    y_scr[:, d_a:] = (ln * jax.nn.sigmoid(ln)).astype(BF16)

    proj = jnp.dot(y_scr[...], w_out_ref[...], preferred_element_type=F32)
    o_ref[...] = x_ref[...] + _rmsnorm(proj, post_g_ref[...])

    for buf, halo in ((cv_buf, HALO_A), (z_buf, HALO_B)):
        for g in range(n_groups):
            _store_rows(buf, g, 0, _load_rows(buf, g, tm, halo))


def _xattn_kernel(step, x_ref, pre_g_ref, w_q_ref, mem_ref, mem_g_ref, w_k_ref, w_v_ref, w_o_ref,
                  post_g_ref, o_ref, kt_ref, v_ref, h_scr, q_scr, o_scr, proj_scr):
    hd = kt_ref.shape[0] // XA_HEADS

    @pl.when(step == 0)
    def _():
        mem_n = _rmsnorm(mem_ref[...], mem_g_ref[...]).astype(BF16)
        k = jnp.dot(mem_n, w_k_ref[...], preferred_element_type=F32)
        v = jnp.dot(mem_n, w_v_ref[...], preferred_element_type=F32)
        kt_ref[...] = k.T.astype(BF16)
        v_ref[...] = v.astype(BF16)

    def pre_norm(rows):
        h_scr[rows, :] = _rmsnorm(x_ref[rows, :], pre_g_ref[...]).astype(BF16)

    def q_proj(rows):
        q = jnp.dot(h_scr[rows, :], w_q_ref[...], preferred_element_type=F32)
        q_scr[rows, :] = (q * (hd ** -0.5)).astype(BF16)

    def heads(rows):
        for i in range(XA_HEADS):
            sl = slice(i * hd, (i + 1) * hd)
            s = jnp.dot(q_scr[rows, sl], kt_ref[sl, :], preferred_element_type=F32)
            e = jnp.exp(s - jnp.max(s, axis=-1, keepdims=True))
            p = e * (1.0 / jnp.sum(e, axis=-1, keepdims=True))
            o_scr[rows, sl] = jnp.dot(p.astype(BF16), v_ref[:, sl], preferred_element_type=F32).astype(BF16)

    def out_proj(rows):
        proj_scr[rows, :] = jnp.dot(o_scr[rows, :], w_o_ref[...], preferred_element_type=F32)

    def post_norm(rows):
        o_ref[rows, :] = x_ref[rows, :] + _rmsnorm(proj_scr[rows, :], post_g_ref[...])

    first, second = _halves(x_ref.shape[0])
    pre_norm(first)
    q_proj(first)
    pre_norm(second)
    heads(first)
    q_proj(second)
    out_proj(first)
    heads(second)
    post_norm(first)
    out_proj(second)
    post_norm(second)


def _ffn_kernel(step, x_ref, pre_g_ref, w_gate_ref, w_up_ref, w_down_ref, post_g_ref, o_ref,
                h_scr, a_scr, proj_scr):
    del step
    d_ff = w_gate_ref.shape[1]

    def pre_norm(rows):
        h_scr[rows, :] = _rmsnorm(x_ref[rows, :], pre_g_ref[...]).astype(BF16)

    def gate_up(rows):
        for c0 in range(0, d_ff, FFN_COLS):
            cols = slice(c0, min(c0 + FFN_COLS, d_ff))
            g = jnp.dot(h_scr[rows, :], w_gate_ref[:, cols], preferred_element_type=F32)
            u = jnp.dot(h_scr[rows, :], w_up_ref[:, cols], preferred_element_type=F32)
            a_scr[rows, cols] = (g * jax.nn.sigmoid(g) * u).astype(BF16)

    def down_proj(rows):
        proj_scr[rows, :] = jnp.dot(a_scr[rows, :], w_down_ref[...], preferred_element_type=F32)

    def post_norm(rows):
        o_ref[rows, :] = x_ref[rows, :] + _rmsnorm(proj_scr[rows, :], post_g_ref[...])

    first, second = _halves(x_ref.shape[0])
    pre_norm(first)
    gate_up(first)
    pre_norm(second)
    down_proj(first)
    gate_up(second)
    post_norm(first)
    down_proj(second)
    post_norm(second)


def _resident(shape):
    return pl.BlockSpec(shape, lambda i: (0,) * len(shape), pipeline_mode=pl.Buffered(1))


class _Bf16Weight:
    def __init__(self, array):
        assert array.ndim == 2 and array.shape[0] % (WEIGHT_CAST_STEPS * BF16_SUBLANES) == 0
        self.array = array
        self.chunk_rows = array.shape[0] // WEIGHT_CAST_STEPS


def _precast_chunks(rows, n_tiles):
    return max(c for c in range(1, n_tiles + 1)
               if n_tiles % c == 0 and rows % (c * BF16_SUBLANES) == 0)


def _row_tiled_call(body, x, consts, scratch_shapes, name, row_tile, precast=()):
    s, d = x.shape
    assert s % row_tile == 0 and row_tile % SUB_ROWS == 0
    n_tiles = s // row_tile
    is_weight = [isinstance(c, _Bf16Weight) for c in consts]
    weights = [c for c in consts if isinstance(c, _Bf16Weight)]
    n_cast = WEIGHT_CAST_STEPS if weights else 0

    def block_index(i):
        return jnp.clip(i - n_cast, 0, n_tiles - 1)

    row_spec = pl.BlockSpec((row_tile, d), lambda i: (block_index(i), 0))

    def const_spec(c):
        if isinstance(c, _Bf16Weight):
            return pl.BlockSpec((c.chunk_rows, c.array.shape[1]), lambda i: (jnp.minimum(i, n_cast - 1), 0))
        return _resident(c.shape)

    def precast_spec(w):
        chunks = _precast_chunks(w.shape[0], n_tiles)
        return pl.BlockSpec((w.shape[0] // chunks, w.shape[1]), lambda i: (block_index(i) * chunks // n_tiles, 0))

    def call_body(x_ref, *refs):
        refs = list(refs)
        const_refs = [refs.pop(0) for _ in consts]
        precast_in = [refs.pop(0) for _ in precast]
        o_ref = refs.pop(0)
        precast_out = [refs.pop(0) for _ in precast]
        weight_scr = [refs.pop(0) for _ in weights]
        chunk_refs = [r for r, w in zip(const_refs, is_weight) if w]
        i = pl.program_id(0)

        @pl.when(i < n_cast)
        def _():
            for c, chunk_ref, w_scr in zip(weights, chunk_refs, weight_scr):
                row0 = pl.multiple_of(i * c.chunk_rows, c.chunk_rows)
                w_scr[pl.ds(row0, c.chunk_rows), :] = chunk_ref[...].astype(BF16)

        @pl.when(i >= n_cast)
        def _():
            cast = iter(weight_scr)
            body_consts = [next(cast) if w else r for r, w in zip(const_refs, is_weight)]
            body(i - n_cast, x_ref, *body_consts, o_ref, *refs)
            for src, dst in zip(precast_in, precast_out):
                dst[...] = src[...].astype(BF16)

    out = pl.pallas_call(
        call_body,
        grid=(n_cast + n_tiles,),
        in_specs=[row_spec] + [const_spec(c) for c in consts] + [precast_spec(w) for w in precast],
        out_specs=[row_spec] + [precast_spec(w) for w in precast],
        out_shape=[jax.ShapeDtypeStruct((s, d), x.dtype)] + [jax.ShapeDtypeStruct(w.shape, BF16) for w in precast],
        scratch_shapes=[pltpu.VMEM(c.array.shape, BF16) for c in weights] + scratch_shapes,
        compiler_params=pltpu.CompilerParams(
            dimension_semantics=("arbitrary",), vmem_limit_bytes=VMEM_LIMIT_BYTES),
        name=name,
    )(x, *[c.array if isinstance(c, _Bf16Weight) else c for c in consts], *precast)
    return out if precast else out[0]


def _row(v):
    return v.reshape(1, -1).astype(F32)


def kernel(x, mem, mix_pre_g, w_mix_in, conv_a_w, conv_b_w, conv_b_b, ln_b_g, ln_b_b, w_mix_out,
           mix_post_g, xa_pre_g, mem_norm_g, w_q, w_k, w_v, w_o, xa_post_g, ffn_pre_g, w_gate, w_up,
           w_down, ffn_post_g):
    bsz, seq, d = x.shape
    assert bsz == 1 and mem.shape[0] == 1
    d_a = conv_a_w.shape[1]
    d_b = conv_b_w.shape[1]
    assert d_a == d_b and w_mix_in.shape[1] == 3 * d_a + 2 * d_b and d_a % (2 * LANES) == 0
    assert conv_a_w.shape[0] == CONV_A_W and conv_b_w.shape[0] == CONV_B_W
    x2 = x.reshape(seq, d)

    mixed = _row_tiled_call(
        _mixer_kernel, x2,
        [_row(mix_pre_g), _Bf16Weight(w_mix_in), conv_a_w, conv_b_w, _row(conv_b_b), _row(ln_b_g),
         _row(ln_b_b), _Bf16Weight(w_mix_out), _row(mix_post_g)],
        [pltpu.VMEM((MIXER_ROW_TILE, d), BF16),
         pltpu.VMEM((MIXER_ROW_TILE, d_a), F32),
         pltpu.VMEM((MIXER_ROW_TILE, d_b), F32),
         pltpu.VMEM((MIXER_ROW_TILE, d_a + d_b), BF16),
         pltpu.VMEM((d_a // LANES, ROW_PITCH * (HALO_A + MIXER_ROW_TILE), LANES), F32),
         pltpu.VMEM((d_b // LANES, ROW_PITCH * (HALO_B + MIXER_ROW_TILE), LANES), F32)],
        "mixer", MIXER_ROW_TILE, precast=(w_q, w_k, w_v, w_o))
    x2, w_q, w_k, w_v, w_o = mixed

    n_mem = mem.shape[1]
    e = w_k.shape[1]
    attended = _row_tiled_call(
        _xattn_kernel, x2,
        [_row(xa_pre_g), w_q, mem.reshape(n_mem, d), _row(mem_norm_g), w_k, w_v, w_o, _row(xa_post_g)],
        [pltpu.VMEM((e, n_mem), BF16),
         pltpu.VMEM((n_mem, e), BF16),
         pltpu.VMEM((ROW_TILE, d), BF16),
         pltpu.VMEM((ROW_TILE, e), BF16),
         pltpu.VMEM((ROW_TILE, e), BF16),
         pltpu.VMEM((ROW_TILE, d), F32)],
        "xattn", ROW_TILE, precast=(w_gate, w_up, w_down))
    x2, w_gate, w_up, w_down = attended

    d_ff = w_gate.shape[1]
    x2 = _row_tiled_call(
        _ffn_kernel, x2,
        [_row(ffn_pre_g), w_gate, w_up, w_down, _row(ffn_post_g)],
        [pltpu.VMEM((ROW_TILE, d), BF16),
         pltpu.VMEM((ROW_TILE, d_ff), BF16),
         pltpu.VMEM((ROW_TILE, d), F32)],
        "ffn", ROW_TILE)
    return x2.reshape(bsz, seq, d)
```

```python
import jax
import jax.numpy as jnp
from jax import lax
from jax.experimental import pallas as pl
from jax.experimental.pallas import tpu as pltpu

RMS_EPS = 1e-6
LN_EPS = 1e-5
CONV_A_W = 3
CONV_B_W = 31
XA_HEADS = 4

ROW_TILE = 1024
SUB_ROWS = 512
MIXER_ROW_TILE = SUB_ROWS
HALO_A = 8
HALO_B = 32
LANES = 128
ROW_PITCH = 2
CONV_ROWS = 128
FFN_COLS = 1024
WEIGHT_CAST_STEPS = 4
BF16_SUBLANES = 16
VMEM_LIMIT_BYTES = 58 * 1024 * 1024

F32 = jnp.float32
BF16 = jnp.bfloat16


def _rmsnorm(x, g):
    ms = jnp.mean(x * x, axis=-1, keepdims=True)
    return x * lax.rsqrt(ms + RMS_EPS) * g


def _halves(tm):
    return [slice(r, r + SUB_ROWS) for r in range(0, tm, SUB_ROWS)]


def _store_rows(buf, g, row0, val):
    buf[g, pl.ds(ROW_PITCH * row0, val.shape[0], stride=ROW_PITCH), :] = val


def _load_rows(buf, g, row0, rows):
    return buf[g, pl.ds(ROW_PITCH * row0, rows, stride=ROW_PITCH), :]


def _causal_dwconv(buf, w_ref, halo, g, r0):
    n_taps = w_ref.shape[0]
    acc = None
    for k in range(n_taps):
        win = _load_rows(buf, g, halo - (n_taps - 1) + k + r0, CONV_ROWS)
        term = w_ref[k:k + 1, g * LANES:(g + 1) * LANES] * win
        acc = term if acc is None else acc + term
    return acc


def _mixer_kernel(step, x_ref, pre_g_ref, w_in_ref, ca_w_ref, cb_w_ref, cb_b_ref, ln_g_ref, ln_b_ref,
                  w_out_ref, post_g_ref, o_ref, h_scr, b_scr, cb_scr, y_scr, cv_buf, z_buf):
    tm = x_ref.shape[0]
    d_a = ca_w_ref.shape[1]
    n_groups = d_a // LANES

    @pl.when(step == 0)
    def _():
        for g in range(n_groups):
            _store_rows(cv_buf, g, 0, jnp.zeros((HALO_A, LANES), F32))
            _store_rows(z_buf, g, 0, jnp.zeros((HALO_B, LANES), F32))

    h_scr[...] = _rmsnorm(x_ref[...], pre_g_ref[...]).astype(BF16)

    def in_proj(*col_starts, width=LANES):
        w = jnp.concatenate([w_in_ref[:, c:c + width] for c in col_starts], axis=-1)
        return jnp.dot(h_scr[...], w, preferred_element_type=F32)

    def stage_glu(g):
        glu = in_proj(3 * d_a + g * LANES, 4 * d_a + g * LANES)
        _store_rows(z_buf, g, HALO_B, glu[:, :LANES] * jax.nn.sigmoid(glu[:, LANES:]))

    def stage_cv(g0):
        cv = in_proj(d_a + g0 * LANES, 2 * d_a + g0 * LANES, width=2 * LANES)
        for g in (g0, g0 + 1):
            lo = (g - g0) * LANES
            _store_rows(cv_buf, g, HALO_A, cv[:, lo:lo + LANES] * cv[:, 2 * LANES + lo:3 * LANES + lo])

    def stage_b():
        b_scr[...] = in_proj(0, width=d_a)

    after_group = {0: [lambda: stage_glu(1), lambda: stage_cv(0)],
                   1: [lambda: stage_glu(2), lambda: stage_cv(2)],
                   2: [lambda: stage_glu(3), stage_b],
                   3: []}

    stage_glu(0)
    for g in range(n_groups):
        for r0 in range(0, tm, CONV_ROWS):
            cb_scr[r0:r0 + CONV_ROWS, g * LANES:(g + 1) * LANES] = _causal_dwconv(z_buf, cb_w_ref, HALO_B, g, r0)
        for stage in after_group[g]:
            stage()

    for g in range(n_groups):
        for r0 in range(0, tm, CONV_ROWS):
            rows, lanes = slice(r0, r0 + CONV_ROWS), slice(g * LANES, (g + 1) * LANES)
            y_scr[rows, lanes] = (b_scr[rows, lanes] * _causal_dwconv(cv_buf, ca_w_ref, HALO_A, g, r0)).astype(BF16)

    cb = cb_scr[...] + cb_b_ref[...]
    mu = jnp.mean(cb, axis=-1, keepdims=True)
    cen = cb - mu
    var = jnp.mean(cen * cen, axis=-1, keepdims=True)
    ln = cen * lax.rsqrt(var + LN_EPS) * ln_g_ref[...] + ln_b_ref[...]
    y_scr[:, d_a:] = (ln * jax.nn.sigmoid(ln)).astype(BF16)

    proj = jnp.dot(y_scr[...], w_out_ref[...], preferred_element_type=F32)
    o_ref[...] = x_ref[...] + _rmsnorm(proj, post_g_ref[...])

    for buf, halo in ((cv_buf, HALO_A), (z_buf, HALO_B)):
        for g in range(n_groups):
            _store_rows(buf, g, 0, _load_rows(buf, g, tm, halo))


def _xattn_kernel(step, x_ref, pre_g_ref, w_q_ref, mem_ref, mem_g_ref, w_k_ref, w_v_ref, w_o_ref,
                  post_g_ref, o_ref, kt_ref, v_ref, h_scr, q_scr, o_scr, proj_scr):
    hd = kt_ref.shape[0] // XA_HEADS

    @pl.when(step == 0)
    def _():
        mem_n = _rmsnorm(mem_ref[...], mem_g_ref[...]).astype(BF16)
        k = jnp.dot(mem_n, w_k_ref[...], preferred_element_type=F32)
        v = jnp.dot(mem_n, w_v_ref[...], preferred_element_type=F32)
        kt_ref[...] = k.T.astype(BF16)
        v_ref[...] = v.astype(BF16)

    def pre_norm(rows):
        h_scr[rows, :] = _rmsnorm(x_ref[rows, :], pre_g_ref[...]).astype(BF16)

    def q_proj(rows):
        q = jnp.dot(h_scr[rows, :], w_q_ref[...], preferred_element_type=F32)
        q_scr[rows, :] = (q * (hd ** -0.5)).astype(BF16)

    def heads(rows):
        for i in range(XA_HEADS):
            sl = slice(i * hd, (i + 1) * hd)
            s = jnp.dot(q_scr[rows, sl], kt_ref[sl, :], preferred_element_type=F32)
            e = jnp.exp(s - jnp.max(s, axis=-1, keepdims=True))
            p = e * (1.0 / jnp.sum(e, axis=-1, keepdims=True))
            o_scr[rows, sl] = jnp.dot(p.astype(BF16), v_ref[:, sl], preferred_element_type=F32).astype(BF16)

    def out_proj(rows):
        proj_scr[rows, :] = jnp.dot(o_scr[rows, :], w_o_ref[...], preferred_element_type=F32)

    def post_norm(rows):
        o_ref[rows, :] = x_ref[rows, :] + _rmsnorm(proj_scr[rows, :], post_g_ref[...])

    first, second = _halves(x_ref.shape[0])
    pre_norm(first)
    q_proj(first)
    pre_norm(second)
    heads(first)
    q_proj(second)
    out_proj(first)
    heads(second)
    post_norm(first)
    out_proj(second)
    post_norm(second)


def _ffn_kernel(step, x_ref, pre_g_ref, w_gate_ref, w_up_ref, w_down_ref, post_g_ref, o_ref,
                h_scr, a_scr, proj_scr):
    del step
    d_ff = w_gate_ref.shape[1]

    def pre_norm(rows):
        h_scr[rows, :] = _rmsnorm(x_ref[rows, :], pre_g_ref[...]).astype(BF16)

    def gate_up(rows):
        for c0 in range(0, d_ff, FFN_COLS):
            cols = slice(c0, min(c0 + FFN_COLS, d_ff))
            g = jnp.dot(h_scr[rows, :], w_gate_ref[:, cols], preferred_element_type=F32)
            u = jnp.dot(h_scr[rows, :], w_up_ref[:, cols], preferred_element_type=F32)
            a_scr[rows, cols] = (g * jax.nn.sigmoid(g) * u).astype(BF16)

    def down_proj(rows):
        proj_scr[rows, :] = jnp.dot(a_scr[rows, :], w_down_ref[...], preferred_element_type=F32)

    def post_norm(rows):
        o_ref[rows, :] = x_ref[rows, :] + _rmsnorm(proj_scr[rows, :], post_g_ref[...])

    first, second = _halves(x_ref.shape[0])
    pre_norm(first)
    gate_up(first)
    pre_norm(second)
    down_proj(first)
    gate_up(second)
    post_norm(first)
    down_proj(second)
    post_norm(second)


def _resident(shape):
    return pl.BlockSpec(shape, lambda i: (0,) * len(shape), pipeline_mode=pl.Buffered(1))


class _Bf16Weight:
    def __init__(self, array):
        assert array.ndim == 2 and array.shape[0] % (WEIGHT_CAST_STEPS * BF16_SUBLANES) == 0
        self.array = array
        self.chunk_rows = array.shape[0] // WEIGHT_CAST_STEPS


def _precast_chunks(rows, n_tiles):
    return max(c for c in range(1, n_tiles + 1)
               if n_tiles % c == 0 and rows % (c * BF16_SUBLANES) == 0)


def _row_tiled_call(body, x, consts, scratch_shapes, name, row_tile, precast=()):
    s, d = x.shape
    assert s % row_tile == 0 and row_tile % SUB_ROWS == 0
    n_tiles = s // row_tile
    is_weight = [isinstance(c, _Bf16Weight) for c in consts]
    weights = [c for c in consts if isinstance(c, _Bf16Weight)]
    n_cast = WEIGHT_CAST_STEPS if weights else 0

    def block_index(i):
        return jnp.clip(i - n_cast, 0, n_tiles - 1)

    row_spec = pl.BlockSpec((row_tile, d), lambda i: (block_index(i), 0))

    def const_spec(c):
        if isinstance(c, _Bf16Weight):
            return pl.BlockSpec((c.chunk_rows, c.array.shape[1]), lambda i: (jnp.minimum(i, n_cast - 1), 0))
        return _resident(c.shape)

    def precast_spec(w):
        chunks = _precast_chunks(w.shape[0], n_tiles)
        return pl.BlockSpec((w.shape[0] // chunks, w.shape[1]), lambda i: (block_index(i) * chunks // n_tiles, 0))

    def call_body(x_ref, *refs):
        refs = list(refs)
        const_refs = [refs.pop(0) for _ in consts]
        precast_in = [refs.pop(0) for _ in precast]
        o_ref = refs.pop(0)
        precast_out = [refs.pop(0) for _ in precast]
        weight_scr = [refs.pop(0) for _ in weights]
        chunk_refs = [r for r, w in zip(const_refs, is_weight) if w]
        i = pl.program_id(0)

        @pl.when(i < n_cast)
        def _():
            for c, chunk_ref, w_scr in zip(weights, chunk_refs, weight_scr):
                row0 = pl.multiple_of(i * c.chunk_rows, c.chunk_rows)
                w_scr[pl.ds(row0, c.chunk_rows), :] = chunk_ref[...].astype(BF16)

        @pl.when(i >= n_cast)
        def _():
            cast = iter(weight_scr)
            body_consts = [next(cast) if w else r for r, w in zip(const_refs, is_weight)]
            body(i - n_cast, x_ref, *body_consts, o_ref, *refs)
            for src, dst in zip(precast_in, precast_out):
                dst[...] = src[...].astype(BF16)

    out = pl.pallas_call(
        call_body,
        grid=(n_cast + n_tiles,),
        in_specs=[row_spec] + [const_spec(c) for c in consts] + [precast_spec(w) for w in precast],
        out_specs=[row_spec] + [precast_spec(w) for w in precast],
        out_shape=[jax.ShapeDtypeStruct((s, d), x.dtype)] + [jax.ShapeDtypeStruct(w.shape, BF16) for w in precast],
        scratch_shapes=[pltpu.VMEM(c.array.shape, BF16) for c in weights] + scratch_shapes,
        compiler_params=pltpu.CompilerParams(
            dimension_semantics=("arbitrary",), vmem_limit_bytes=VMEM_LIMIT_BYTES),
        name=name,
    )(x, *[c.array if isinstance(c, _Bf16Weight) else c for c in consts], *precast)
    return out if precast else out[0]


def _row(v):
    return v.reshape(1, -1).astype(F32)


def kernel(x, mem, mix_pre_g, w_mix_in, conv_a_w, conv_b_w, conv_b_b, ln_b_g, ln_b_b, w_mix_out,
           mix_post_g, xa_pre_g, mem_norm_g, w_q, w_k, w_v, w_o, xa_post_g, ffn_pre_g, w_gate, w_up,
           w_down, ffn_post_g):
    bsz, seq, d = x.shape
    assert bsz == 1 and mem.shape[0] == 1
    d_a = conv_a_w.shape[1]
    d_b = conv_b_w.shape[1]
    assert d_a == d_b and w_mix_in.shape[1] == 3 * d_a + 2 * d_b and d_a % (2 * LANES) == 0
    assert conv_a_w.shape[0] == CONV_A_W and conv_b_w.shape[0] == CONV_B_W
    x2 = x.reshape(seq, d)

    mixed = _row_tiled_call(
        _mixer_kernel, x2,
        [_row(mix_pre_g), _Bf16Weight(w_mix_in), conv_a_w, conv_b_w, _row(conv_b_b), _row(ln_b_g),
         _row(ln_b_b), _Bf16Weight(w_mix_out), _row(mix_post_g)],
        [pltpu.VMEM((MIXER_ROW_TILE, d), BF16),
         pltpu.VMEM((MIXER_ROW_TILE, d_a), F32),
         pltpu.VMEM((MIXER_ROW_TILE, d_b), F32),
         pltpu.VMEM((MIXER_ROW_TILE, d_a + d_b), BF16),
         pltpu.VMEM((d_a // LANES, ROW_PITCH * (HALO_A + MIXER_ROW_TILE), LANES), F32),
         pltpu.VMEM((d_b // LANES, ROW_PITCH * (HALO_B + MIXER_ROW_TILE), LANES), F32)],
        "mixer", MIXER_ROW_TILE, precast=(w_q, w_k, w_v, w_o))
    x2, w_q, w_k, w_v, w_o = mixed

    n_mem = mem.shape[1]
    e = w_k.shape[1]
    attended = _row_tiled_call(
        _xattn_kernel, x2,
        [_row(xa_pre_g), w_q, mem.reshape(n_mem, d), _row(mem_norm_g), w_k, w_v, w_o, _row(xa_post_g)],
        [pltpu.VMEM((e, n_mem), BF16),
         pltpu.VMEM((n_mem, e), BF16),
         pltpu.VMEM((ROW_TILE, d), BF16),
         pltpu.VMEM((ROW_TILE, e), BF16),
         pltpu.VMEM((ROW_TILE, e), BF16),
         pltpu.VMEM((ROW_TILE, d), F32)],
        "xattn", ROW_TILE, precast=(w_gate, w_up, w_down))
    x2, w_gate, w_up, w_down = attended

    d_ff = w_gate.shape[1]
    x2 = _row_tiled_call(
        _ffn_kernel, x2,
        [_row(ffn_pre_g), w_gate, w_up, w_down, _row(ffn_post_g)],
        [pltpu.VMEM((ROW_TILE, d), BF16),
         pltpu.VMEM((ROW_TILE, d_ff), BF16),
         pltpu.VMEM((ROW_TILE, d), F32)],
        "ffn", ROW_TILE)
    return x2.reshape(bsz, seq, d)
```
